```python
import math, functools
import jax, jax.numpy as jnp
from jax import lax
import numpy as np

D_MODEL = 4096
BATCH = 4
SEQ = 2048
DEPTH = 4
DEC_BATCH = 128
DEC_SEQ = 1
PAST_LEN = 8192
PAGE_SIZE = 128

SSD_INNER = D_MODEL
SSD_HEADDIM = 64
SSD_HEADS = SSD_INNER // SSD_HEADDIM
SSD_GROUPS = 8
SSD_STATE = 128
SSD_HPG = SSD_HEADS // SSD_GROUPS
CONV_W = 4
CONV_DIM = SSD_INNER + 2 * SSD_GROUPS * SSD_STATE
SSD_CHUNK = 128
MLA_HEADS = D_MODEL // 128
QK_NOPE = 128
QK_ROPE = 64
V_DIM = 128
Q_LORA = 768
KV_LORA = 512
MLA_WIDTH = MLA_HEADS * V_DIM
ROPE_THETA = 10000.0
Q_BLOCK = 128
PLE_DIM = 256
PLE_GATE_RANK = 256
EPS = 1e-6
IN_SIZES = (SSD_INNER, CONV_DIM, SSD_HEADS, MLA_WIDTH, Q_LORA, KV_LORA, QK_ROPE, D_MODEL, D_MODEL)
D_IN = sum(IN_SIZES)

kernel_name = 'hybrid_ssd_mla_ple_decode_step'


def rmsnorm(x, g):
    xf = x.astype(jnp.float32)
    xf = xf * lax.rsqrt(jnp.mean(xf * xf, axis=-1, keepdims=True) + EPS)
    return xf.astype(x.dtype) * g


def gated_group_rmsnorm(y, z, g):
    v = (y * jax.nn.silu(z)).astype(jnp.float32)
    v = v.reshape(v.shape[:-1] + (SSD_GROUPS, SSD_INNER // SSD_GROUPS))
    v = v * lax.rsqrt(jnp.mean(v * v, axis=-1, keepdims=True) + EPS)
    return v.reshape(y.shape).astype(y.dtype) * g


def split_cols(u):
    offs = np.cumsum(np.array(IN_SIZES))[:-1].tolist()
    return jnp.split(u, offs, axis=-1)


def rope_tables(pos):
    half = QK_ROPE // 2
    inv = ROPE_THETA ** (-jnp.arange(half, dtype=jnp.float32) / half)
    ang = pos.astype(jnp.float32)[:, None] * inv[None, :]
    return jnp.cos(ang), jnp.sin(ang)


def apply_rope(x, cos, sin):
    half = QK_ROPE // 2
    cos = cos.astype(x.dtype)
    sin = sin.astype(x.dtype)
    x1, x2 = x[..., :half], x[..., half:]
    return jnp.concatenate([x1 * cos - x2 * sin, x2 * cos + x1 * sin], axis=-1)


def causal_conv(xbc, conv_state, w, b):
    t = xbc.shape[1]
    xp = jnp.concatenate([conv_state.astype(xbc.dtype), xbc], axis=1)
    y = b
    for k in range(CONV_W):
        y = y + xp[:, k:k + t] * w[k]
    return jax.nn.silu(y), xp[:, xp.shape[1] - (CONV_W - 1):]


def ssd_scan(x, dt, a, bm, cm, h0):
    b, t = x.shape[:2]
    q = min(SSD_CHUNK, t)
    nc = -(-t // q)
    pad = nc * q - t
    if pad:
        padt = lambda v: jnp.pad(v, [(0, 0), (0, pad)] + [(0, 0)] * (v.ndim - 2))
        x, dt, bm, cm = padt(x), padt(dt), padt(bm), padt(cm)
    xc = x.reshape(b, nc, q, SSD_GROUPS, SSD_HPG, SSD_HEADDIM)
    dtc = dt.reshape(b, nc, q, SSD_GROUPS, SSD_HPG)
    bc = bm.reshape(b, nc, q, SSD_GROUPS, SSD_STATE)
    cc = cm.reshape(b, nc, q, SSD_GROUPS, SSD_STATE)
    cum = jnp.cumsum(dtc * a.reshape(SSD_GROUPS, SSD_HPG), axis=2)
    tri = jnp.tril(jnp.ones((q, q), dtype=bool))
    diff = cum[:, :, :, None] - cum[:, :, None, :]
    lmat = jnp.exp(jnp.where(tri[:, :, None, None], diff, -jnp.inf))
    cb = jnp.einsum('bclgn,bcsgn->bclsg', cc, bc)
    m = cb[..., None] * lmat * dtc[:, :, None]
    y_diag = jnp.einsum('bclsgr,bcsgrp->bclgrp', m, xc)
    decay_s = jnp.exp(cum[:, :, -1:] - cum) * dtc
    states = jnp.einsum('bclgn,bclgr,bclgrp->bcgrpn', bc, decay_s, xc)
    block_decay = jnp.exp(cum[:, :, -1])

    def step(h, inp):
        st, dec = inp
        return h * dec[..., None, None] + st, h

    h_init = h0.reshape(b, SSD_GROUPS, SSD_HPG, SSD_HEADDIM, SSD_STATE).astype(jnp.float32)
    h_last, h_prev = lax.scan(step, h_init, (jnp.moveaxis(states, 1, 0).astype(jnp.float32),
                                             jnp.moveaxis(block_decay, 1, 0)))
    h_prev = jnp.moveaxis(h_prev, 0, 1)
    y_off = jnp.einsum('bclgn,bcgrpn,bclgr->bclgrp', cc, h_prev, jnp.exp(cum))
    y = (y_diag + y_off).reshape(b, nc * q, SSD_HEADS, SSD_HEADDIM)[:, :t]
    return y, h_last.reshape(b, SSD_HEADS, SSD_HEADDIM, SSD_STATE).astype(h0.dtype)


def mla_prompt_attention(q_nope, q_rope, ckv, kr, w_uk, w_uv):
    b, s = q_nope.shape[:2]
    qb = min(Q_BLOCK, s)
    nb = s // qb
    scale = 1.0 / math.sqrt(QK_NOPE + QK_ROPE)
    k_nope = jnp.einsum('bsc,chd->bshd', ckv, w_uk)
    v = jnp.einsum('bsc,chd->bshd', ckv, w_uv)
    qn = q_nope.reshape(b, nb, qb, MLA_HEADS, QK_NOPE).swapaxes(0, 1)
    qr = q_rope.reshape(b, nb, qb, MLA_HEADS, QK_ROPE).swapaxes(0, 1)
    kpos = jnp.arange(s)

    def block(args):
        qn_b, qr_b, i = args
        sc = jnp.einsum('bqhd,bkhd->bhqk', qn_b, k_nope) + jnp.einsum('bqhr,bkr->bhqk', qr_b, kr)
        sc = sc.astype(jnp.float32) * scale
        qpos = i * qb + jnp.arange(qb)
        sc = jnp.where(kpos[None, :] <= qpos[:, None], sc, -jnp.inf)
        p = jax.nn.softmax(sc, axis=-1).astype(v.dtype)
        return jnp.einsum('bhqk,bkhd->bqhd', p, v)

    o = lax.map(block, (qn, qr, jnp.arange(nb)))
    return o.swapaxes(0, 1).reshape(b, s, MLA_HEADS, V_DIM)


def mla_sample_attention(q_nope, q_rope, ckv_new, kr_new, w_uk, w_uv, cache_ckv, cache_krope, page_table, layer):
    bd, t = q_nope.shape[:2]
    scale = 1.0 / math.sqrt(QK_NOPE + QK_ROPE)
    past_ckv = cache_ckv[layer, page_table].reshape(bd, -1, KV_LORA)
    past_kr = cache_krope[layer, page_table].reshape(bd, -1, QK_ROPE)
    n_past = past_ckv.shape[1]
    q_lat = jnp.einsum('bthd,chd->bthc', q_nope, w_uk)
    s_past = jnp.einsum('bthc,blc->bhtl', q_lat, past_ckv) + jnp.einsum('bthr,blr->bhtl', q_rope, past_kr)
    s_new = jnp.einsum('bthc,bsc->bhts', q_lat, ckv_new) + jnp.einsum('bthr,bsr->bhts', q_rope, kr_new)
    s_new = jnp.where(jnp.tril(jnp.ones((t, t), dtype=bool)), s_new, -jnp.inf)
    sc = jnp.concatenate([s_past, s_new], axis=-1).astype(jnp.float32) * scale
    p = jax.nn.softmax(sc, axis=-1).astype(q_lat.dtype)
    o_lat = (jnp.einsum('bhtl,blc->bthc', p[..., :n_past], past_ckv)
             + jnp.einsum('bhts,bsc->bthc', p[..., n_past:], ckv_new))
    return jnp.einsum('bthc,chd->bthd', o_lat, w_uv)


def layer(x, p_l, cos, sin, conv_state, ssm_state, attend, lw):
    (g_norm, w_in, conv_w, conv_b, dt_bias, a_log, d_skip, g_ssd, w_ssd_o,
     g_q, w_q_b, g_kv, w_kv_b, w_mla_o, w_out, g_ple, w_ple, w_pg_down, w_pg_up) = lw
    b, t, _ = x.shape
    h = rmsnorm(x, g_norm)
    z_s, xbc, dt_raw, z_m, cq, ckv, kr, g_s, g_m = split_cols(h @ w_in)
    xbc, conv_new = causal_conv(xbc, conv_state, conv_w, conv_b)
    xs, bm, cm = jnp.split(xbc, [SSD_INNER, SSD_INNER + SSD_GROUPS * SSD_STATE], axis=-1)
    xs = xs.reshape(b, t, SSD_HEADS, SSD_HEADDIM)
    dt = jax.nn.softplus(dt_raw.astype(jnp.float32) + dt_bias.astype(jnp.float32))
    a = -jnp.exp(a_log.astype(jnp.float32))
    y, ssm_new = ssd_scan(xs, dt, a, bm.reshape(b, t, SSD_GROUPS, SSD_STATE),
                          cm.reshape(b, t, SSD_GROUPS, SSD_STATE), ssm_state)
    y = (y + d_skip[:, None] * xs).astype(x.dtype).reshape(b, t, SSD_INNER)
    y_s = gated_group_rmsnorm(y, z_s, g_ssd) @ w_ssd_o
    q = (rmsnorm(cq, g_q) @ w_q_b).reshape(b, t, MLA_HEADS, QK_NOPE + QK_ROPE)
    q_nope = q[..., :QK_NOPE]
    q_rope = apply_rope(q[..., QK_NOPE:], cos[:, None], sin[:, None])
    ckv = rmsnorm(ckv, g_kv)
    kr = apply_rope(kr, cos, sin)
    w_kv = w_kv_b.reshape(KV_LORA, MLA_HEADS, QK_NOPE + V_DIM)
    o = attend(q_nope, q_rope, ckv, kr, w_kv[..., :QK_NOPE], w_kv[..., QK_NOPE:])
    y_m = (o.reshape(b, t, MLA_WIDTH) * jax.nn.silu(z_m)) @ w_mla_o
    x = x + (jax.nn.sigmoid(g_s) * y_s + jax.nn.sigmoid(g_m) * y_m) @ w_out
    gate = jax.nn.sigmoid(jax.nn.gelu(rmsnorm(x, g_ple) @ w_pg_down) @ w_pg_up)
    x = x + gate * (p_l @ w_ple)
    return x, ckv, kr, ssm_new, conv_new


def setup_inputs(seed: int = 0) -> dict:
    key = jax.random.key(seed)
    ks = jax.random.split(key, 32)
    f32 = jnp.float32
    n_pages = PAST_LEN // PAGE_SIZE
    n_used = DEC_BATCH * n_pages
    n_pool = n_used + max(1, n_used // 4)
    nrm = lambda k, shape, s: jax.random.normal(k, shape, f32) * s
    gain = lambda k, shape: 1.0 + 0.01 * jax.random.normal(k, shape, f32)
    dt0 = jnp.exp(jax.random.uniform(ks[12], (DEPTH, SSD_HEADS), f32, math.log(1e-3), math.log(1e-1)))
    return {
        'x_prompt': nrm(ks[0], (BATCH, SEQ, D_MODEL), 1.0),
        'x_sample': nrm(ks[1], (DEC_BATCH, DEC_SEQ, D_MODEL), 1.0),
        'p_prompt': nrm(ks[2], (DEPTH, BATCH, SEQ, PLE_DIM), 1.0),
        'p_sample': nrm(ks[3], (DEPTH, DEC_BATCH, DEC_SEQ, PLE_DIM), 1.0),
        'cache_ckv': nrm(ks[4], (DEPTH, n_pool, PAGE_SIZE, KV_LORA), 1.0),
        'cache_krope': nrm(ks[5], (DEPTH, n_pool, PAGE_SIZE, QK_ROPE), 1.0),
        'state_ssm': nrm(ks[6], (DEPTH, DEC_BATCH, SSD_HEADS, SSD_HEADDIM, SSD_STATE), 0.1),
        'state_conv': nrm(ks[7], (DEPTH, DEC_BATCH, CONV_W - 1, CONV_DIM), 1.0),
        'page_table': jax.random.permutation(ks[8], n_pool)[:n_used].reshape(DEC_BATCH, n_pages).astype(jnp.int32),
        'g_norm': gain(ks[9], (DEPTH, D_MODEL)),
        'w_in': nrm(ks[10], (DEPTH, D_MODEL, D_IN), D_MODEL ** -0.5),
        'conv_w': nrm(ks[11], (DEPTH, CONV_W, CONV_DIM), CONV_W ** -0.5),
        'conv_b': nrm(ks[13], (DEPTH, CONV_DIM), 0.01),
        'dt_bias': dt0 + jnp.log(-jnp.expm1(-dt0)),
        'a_log': jnp.log(jax.random.uniform(ks[14], (DEPTH, SSD_HEADS), f32, 1.0, 16.0)),
        'd_skip': gain(ks[15], (DEPTH, SSD_HEADS)),
        'g_ssd': gain(ks[16], (DEPTH, SSD_INNER)),
        'w_ssd_o': nrm(ks[17], (DEPTH, SSD_INNER, D_MODEL), SSD_INNER ** -0.5),
        'g_q': gain(ks[18], (DEPTH, Q_LORA)),
        'w_q_b': nrm(ks[19], (DEPTH, Q_LORA, MLA_HEADS * (QK_NOPE + QK_ROPE)), Q_LORA ** -0.5),
        'g_kv': gain(ks[20], (DEPTH, KV_LORA)),
        'w_kv_b': nrm(ks[21], (DEPTH, KV_LORA, MLA_HEADS * (QK_NOPE + V_DIM)), KV_LORA ** -0.5),
        'w_mla_o': nrm(ks[22], (DEPTH, MLA_WIDTH, D_MODEL), MLA_WIDTH ** -0.5),
        'w_out': nrm(ks[23], (DEPTH, D_MODEL, D_MODEL), D_MODEL ** -0.5),
        'g_ple': gain(ks[24], (DEPTH, D_MODEL)),
        'w_ple': nrm(ks[25], (DEPTH, PLE_DIM, D_MODEL), PLE_DIM ** -0.5),
        'w_pg_down': nrm(ks[26], (DEPTH, D_MODEL, PLE_GATE_RANK), D_MODEL ** -0.5),
        'w_pg_up': nrm(ks[27], (DEPTH, PLE_GATE_RANK, D_MODEL), PLE_GATE_RANK ** -0.5),
        'g_final': gain(ks[28], (D_MODEL,)),
    }


def reference(x_prompt, x_sample, p_prompt, p_sample, cache_ckv, cache_krope, state_ssm, state_conv,
              page_table, g_norm, w_in, conv_w, conv_b, dt_bias, a_log, d_skip, g_ssd, w_ssd_o,
              g_q, w_q_b, g_kv, w_kv_b, w_mla_o, w_out, g_ple, w_ple, w_pg_down, w_pg_up, g_final):
    b_p, s_p = x_prompt.shape[:2]
    b_s, s_s = x_sample.shape[:2]
    past_len = page_table.shape[1] * cache_ckv.shape[2]
    cos_p, sin_p = rope_tables(jnp.arange(s_p))
    cos_s, sin_s = rope_tables(past_len + jnp.arange(s_s))
    conv0 = jnp.zeros((b_p, CONV_W - 1, CONV_DIM), x_prompt.dtype)
    ssm0 = jnp.zeros((b_p, SSD_HEADS, SSD_HEADDIM, SSD_STATE), jnp.float32)
    hp, hs = x_prompt, x_sample
    ckv_p, kr_p, ssm_p, conv_p = [], [], [], []
    ckv_s, kr_s, ssm_s, conv_s = [], [], [], []
    for i in range(DEPTH):
        lw = (g_norm[i], w_in[i], conv_w[i], conv_b[i], dt_bias[i], a_log[i], d_skip[i], g_ssd[i],
              w_ssd_o[i], g_q[i], w_q_b[i], g_kv[i], w_kv_b[i], w_mla_o[i], w_out[i], g_ple[i],
              w_ple[i], w_pg_down[i], w_pg_up[i])
        hp, c1, r1, s1, v1 = layer(hp, p_prompt[i], cos_p, sin_p, conv0, ssm0, mla_prompt_attention, lw)
        attend_s = functools.partial(mla_sample_attention, cache_ckv=cache_ckv, cache_krope=cache_krope,
                                     page_table=page_table, layer=i)
        hs, c2, r2, s2, v2 = layer(hs, p_sample[i], cos_s, sin_s, state_conv[i], state_ssm[i], attend_s, lw)
        ckv_p.append(c1); kr_p.append(r1); ssm_p.append(s1); conv_p.append(v1)
        ckv_s.append(c2); kr_s.append(r2); ssm_s.append(s2); conv_s.append(v2)
    y_prompt = rmsnorm(hp, g_final)
    y_sample = rmsnorm(hs, g_final)
    return (y_prompt, y_sample,
            jnp.stack(ckv_p), jnp.stack(kr_p), jnp.stack(ssm_p), jnp.stack(conv_p),
            jnp.stack(ckv_s), jnp.stack(kr_s), jnp.stack(ssm_s), jnp.stack(conv_s))
```

```python
import functools
import math

import jax
import jax.numpy as jnp
import numpy as np
from jax import lax
from jax.experimental import pallas as pl
from jax.experimental.pallas import tpu as pltpu

F32 = jnp.float32
BF16 = jnp.bfloat16
HIGHEST = lax.Precision.HIGHEST

D_MODEL = 4096
SSD_INNER = 4096
SSD_HEADDIM = 64
SSD_HEADS = 64
SSD_GROUPS = 8
SSD_STATE = 128
SSD_HPG = SSD_HEADS // SSD_GROUPS
SSD_GW = SSD_HPG * SSD_HEADDIM
CONV_W = 4
CONV_DIM = SSD_INNER + 2 * SSD_GROUPS * SSD_STATE
SSD_CHUNK = 128
MLA_HEADS = 32
QK_NOPE = 128
QK_ROPE = 64
V_DIM = 128
Q_LORA = 768
KV_LORA = 512
MLA_WIDTH = MLA_HEADS * V_DIM
ROPE_THETA = 10000.0
PLE_DIM = 256
EPS = 1e-6
LANES = 128
SUBLANES = 8
QK_PAD = 2 * LANES
MID_W = Q_LORA + KV_LORA + LANES
QLAT_W = KV_LORA + LANES
PAGES_PER_STEP = 8
VMEM_LIMIT = 56 * 1024 * 1024

_IN_SIZES = (SSD_INNER, CONV_DIM, SSD_HEADS, MLA_WIDTH, Q_LORA, KV_LORA, QK_ROPE, D_MODEL, D_MODEL)
_IN_OFFS = tuple(int(v) for v in np.cumsum((0,) + _IN_SIZES))
ZM_COL0 = SSD_INNER + CONV_DIM


def _cparams(*sem):
    return pltpu.CompilerParams(dimension_semantics=sem, vmem_limit_bytes=VMEM_LIMIT)


def _sigmoid(x):
    return 1.0 / (1.0 + jnp.exp(-x))


def _silu(x):
    return x * _sigmoid(x)


def _softplus(x):
    return jnp.maximum(x, 0.0) + jnp.log(1.0 + jnp.exp(-jnp.abs(x)))


def _rope128(x, cos_t, sin_t):
    lane = lax.broadcasted_iota(jnp.int32, x.shape, 1)
    rot = jnp.where(lane < QK_ROPE // 2, -pltpu.roll(x, LANES - QK_ROPE // 2, 1),
                    pltpu.roll(x, QK_ROPE // 2, 1))
    return x * cos_t + rot * sin_t


def _dot_nt(a, b):
    return lax.dot_general(a, b, (((1,), (1,)), ((), ())), preferred_element_type=F32)


def _rmsnorm_kernel(x_ref, g_ref, o_ref):
    x = x_ref[...]
    y = x * lax.rsqrt(jnp.mean(x * x, axis=-1, keepdims=True) + EPS)
    o_ref[...] = (y * g_ref[...]).astype(o_ref.dtype)


def _rmsnorm(x, g, out_dtype=BF16):
    m, k = x.shape
    tm = min(m, 256)
    return pl.pallas_call(
        _rmsnorm_kernel,
        out_shape=jax.ShapeDtypeStruct((m, k), out_dtype),
        grid=(m // tm,),
        in_specs=[pl.BlockSpec((tm, k), lambda i: (i, 0)),
                  pl.BlockSpec((1, k), lambda i: (0, 0))],
        out_specs=pl.BlockSpec((tm, k), lambda i: (i, 0)),
        compiler_params=_cparams("parallel"),
        name="rmsnorm",
    )(x, g.reshape(1, k))


def _mm_kernel(a_ref, w_ref, *refs, mode):
    o_ref = refs[-1]
    acc = jnp.dot(a_ref[...], w_ref[...], preferred_element_type=F32)
    if mode == "plain":
        out = acc
    elif mode == "gelu":
        out = jax.nn.gelu(acc)
    elif mode == "sig_mul":
        out = _sigmoid(refs[0][...]) * acc
    elif mode == "sig_mul_add":
        out = _sigmoid(refs[0][...]) * acc + refs[1][...]
    elif mode == "silu_mul":
        out = _silu(refs[0][...]) * acc
    elif mode == "residual":
        out = refs[0][...] + acc
    elif mode == "rope_q":
        cos_t, sin_t = refs[0][...], refs[1][...]
        for c in range(acc.shape[1] // QK_PAD):
            lo = c * QK_PAD
            o_ref[:, lo:lo + LANES] = acc[:, lo:lo + LANES].astype(o_ref.dtype)
            o_ref[:, lo + LANES:lo + QK_PAD] = _rope128(
                acc[:, lo + LANES:lo + QK_PAD], cos_t, sin_t).astype(o_ref.dtype)
        return
    elif mode == "kcat":
        krp = refs[0][...]
        for c in range(acc.shape[1] // LANES):
            o_ref[:, c * QK_PAD:c * QK_PAD + LANES] = acc[:, c * LANES:(c + 1) * LANES].astype(o_ref.dtype)
            o_ref[:, c * QK_PAD + LANES:(c + 1) * QK_PAD] = krp
        return
    else:
        raise ValueError(mode)
    o_ref[...] = out.astype(o_ref.dtype)


def _mm(a, w, *, mode="plain", extras=(), extra_col0=(), out_dtype=F32, tm=None, tn=512, a_col0=0,
        table_rows=None):
    m = a.shape[0]
    k, n = w.shape
    if tm is None:
        tm = min(m, 1024)
    tn = min(tn, n)
    assert m % tm == 0 and a_col0 % k == 0
    grid = (m // tm, pl.cdiv(n, tn))
    a_blk = a_col0 // k
    in_specs = [pl.BlockSpec((tm, k), lambda i, j: (i, a_blk)),
                pl.BlockSpec((k, tn), lambda i, j: (0, j))]
    out_n, out_tn = n, tn
    if mode == "rope_q":
        assert tn % QK_PAD == 0 and table_rows % tm == 0
        nt = table_rows // tm
        in_specs += [pl.BlockSpec((tm, LANES), lambda i, j: (i % nt, 0))] * 2
    elif mode == "kcat":
        in_specs += [pl.BlockSpec((tm, LANES), lambda i, j: (i, 0))]
        out_n, out_tn = 2 * n, 2 * tn
    else:
        for c0 in (tuple(extra_col0) + (0,) * len(extras))[:len(extras)]:
            in_specs.append(pl.BlockSpec((tm, tn), functools.partial(lambda i, j, c: (i, j + c), c=c0)))
    return pl.pallas_call(
        functools.partial(_mm_kernel, mode=mode),
        out_shape=jax.ShapeDtypeStruct((m, out_n), out_dtype),
        grid=grid,
        in_specs=in_specs,
        out_specs=pl.BlockSpec((tm, out_tn), lambda i, j: (i, j)),
        compiler_params=_cparams("parallel", "arbitrary"),
        name="mm_" + mode,
    )(a, w, *extras)


def _mid_kernel(u_ref, gq_ref, gkv_ref, cos_ref, sin_ref,
                cqn_ref, ckv_ref, ckvb_ref, kr_ref, krp_ref, dt_ref):
    def norm(x, g):
        return x * lax.rsqrt(jnp.mean(x * x, axis=-1, keepdims=True) + EPS) * g

    cqn_ref[...] = norm(u_ref[:, :Q_LORA], gq_ref[...]).astype(cqn_ref.dtype)
    ckv = norm(u_ref[:, Q_LORA:Q_LORA + KV_LORA], gkv_ref[...])
    ckv_ref[...] = ckv
    ckvb_ref[...] = ckv.astype(ckvb_ref.dtype)
    krdt = u_ref[:, Q_LORA + KV_LORA:]
    lane = lax.broadcasted_iota(jnp.int32, krdt.shape, 1)
    kr = _rope128(jnp.where(lane < QK_ROPE, krdt, 0.0), cos_ref[...], sin_ref[...])
    kr_ref[...] = kr[:, :QK_ROPE]
    krp_ref[...] = kr.astype(krp_ref.dtype)
    dt_ref[...] = pltpu.roll(krdt, QK_ROPE, 1)[:, :SSD_HEADS]


def _mid_post(u_mid, g_q, g_kv, cos_t, sin_t):
    m = u_mid.shape[0]
    table_rows = cos_t.shape[0]
    tm = min(m, table_rows, 256)
    nt = table_rows // tm
    row = lambda i: (i, 0)
    return pl.pallas_call(
        _mid_kernel,
        out_shape=(jax.ShapeDtypeStruct((m, Q_LORA), BF16),
                   jax.ShapeDtypeStruct((m, KV_LORA), F32),
                   jax.ShapeDtypeStruct((m, KV_LORA), BF16),
                   jax.ShapeDtypeStruct((m, QK_ROPE), F32),
                   jax.ShapeDtypeStruct((m, LANES), BF16),
                   jax.ShapeDtypeStruct((m, SSD_HEADS), F32)),
        grid=(m // tm,),
        in_specs=[pl.BlockSpec((tm, MID_W), row),
                  pl.BlockSpec((1, Q_LORA), lambda i: (0, 0)),
                  pl.BlockSpec((1, KV_LORA), lambda i: (0, 0)),
                  pl.BlockSpec((tm, LANES), lambda i: (i % nt, 0)),
                  pl.BlockSpec((tm, LANES), lambda i: (i % nt, 0))],
        out_specs=(pl.BlockSpec((tm, Q_LORA), row), pl.BlockSpec((tm, KV_LORA), row),
                   pl.BlockSpec((tm, KV_LORA), row), pl.BlockSpec((tm, QK_ROPE), row),
                   pl.BlockSpec((tm, LANES), row), pl.BlockSpec((tm, SSD_HEADS), row)),
        compiler_params=_cparams("parallel"),
        name="mid_post",
    )(u_mid, g_q.reshape(1, -1), g_kv.reshape(1, -1), cos_t, sin_t)


def _conv_kernel(x_ref, halo_ref, w_ref, b_ref, o_ref, *, tiles_per_seq):
    i = pl.program_id(1)
    x = x_ref[...]
    halo = jnp.where(i % tiles_per_seq == 0, 0.0, halo_ref[...])
    ext = jnp.concatenate([halo, x], axis=0)
    w = w_ref[...]
    y = b_ref[...] + x * w[CONV_W - 1:CONV_W, :]
    for s in range(1, CONV_W):
        y = y + pltpu.roll(ext, s, 0)[SUBLANES:, :] * w[CONV_W - 1 - s:CONV_W - s, :]
    o_ref[...] = _silu(y).astype(o_ref.dtype)


def _conv_prompt(u_a, conv_w, conv_b, seq):
    m = u_a.shape[0]
    tt = min(seq, 512)
    tc = 512
    c0 = SSD_INNER // tc
    hb = tt // SUBLANES
    return pl.pallas_call(
        functools.partial(_conv_kernel, tiles_per_seq=seq // tt),
        out_shape=jax.ShapeDtypeStruct((m, CONV_DIM), F32),
        grid=(CONV_DIM // tc, m // tt),
        in_specs=[pl.BlockSpec((tt, tc), lambda j, i: (i, c0 + j)),
                  pl.BlockSpec((SUBLANES, tc), lambda j, i: (jnp.maximum(i * hb - 1, 0), c0 + j)),
                  pl.BlockSpec((CONV_W, tc), lambda j, i: (0, j)),
                  pl.BlockSpec((1, tc), lambda j, i: (0, j))],
        out_specs=pl.BlockSpec((tt, tc), lambda j, i: (i, j)),
        compiler_params=_cparams("parallel", "arbitrary"),
        name="conv_prompt",
    )(u_a, u_a, conv_w, conv_b.reshape(1, -1))


def _ssd_kernel(x_ref, b_ref, c_ref, z_ref, dt_ref, dtt_ref, bias_ref, alog_ref, biast_ref, alogt_ref,
                dskip_ref, g_ref, e_ref, y_ref, st_out_ref, st_ref, yd_ref):
    c = pl.program_id(1)
    q = SSD_CHUNK

    @pl.when(c == 0)
    def _():
        st_ref[...] = jnp.zeros_like(st_ref)

    dt = _softplus(dt_ref[...] + bias_ref[...])
    dta = dt * -jnp.exp(alog_ref[...])
    dtt = _softplus(dtt_ref[...] + biast_ref[...])
    dtat = dtt * -jnp.exp(alogt_ref[...])
    row = lax.broadcasted_iota(jnp.int32, (q, q), 0)
    col = lax.broadcasted_iota(jnp.int32, (q, q), 1)
    tri = row >= col
    cum = jnp.dot(tri.astype(F32), dta, precision=HIGHEST, preferred_element_type=F32)
    cumt = jnp.dot(dtat, (row <= col).astype(F32), precision=HIGHEST, preferred_element_type=F32)
    cum_last = cum[q - 1:q, :]
    e = e_ref[...]

    def expand(v):
        return jnp.dot(v, e, precision=HIGHEST, preferred_element_type=F32)

    e_cum = expand(jnp.exp(cum))
    dw = expand(jnp.exp(cum_last - cum) * dt)
    dec = expand(jnp.broadcast_to(jnp.exp(cum_last), (SUBLANES, SSD_HEADS)))[0:1, :]

    for g in range(SSD_GROUPS):
        gl = slice(g * SSD_GW, (g + 1) * SSD_GW)
        nl = slice(g * SSD_STATE, (g + 1) * SSD_STATE)
        bg = b_ref[:, nl]
        cg = c_ref[:, nl].astype(BF16)
        cb = _dot_nt(cg, bg.astype(BF16))
        xg = x_ref[:, gl]
        xg_b = xg.astype(BF16)
        st_g = st_ref[:, gl]
        y_off = jnp.dot(cg, st_g.astype(BF16), preferred_element_type=F32) * e_cum[:, gl]
        for r in range(SSD_HPG):
            h = g * SSD_HPG + r
            lmat = jnp.exp(jnp.where(tri, cum[:, h:h + 1] - cumt[h:h + 1, :], -jnp.inf))
            mm = (cb * lmat * dtt[h:h + 1, :]).astype(BF16)
            yd_ref[:, r * SSD_HEADDIM:(r + 1) * SSD_HEADDIM] = jnp.dot(
                mm, xg_b[:, r * SSD_HEADDIM:(r + 1) * SSD_HEADDIM], preferred_element_type=F32)
        xw = (xg * dw[:, gl]).astype(BF16)
        st_ref[:, gl] = st_g * dec[:, gl] + jnp.dot(bg.T.astype(BF16), xw, preferred_element_type=F32)
        y = yd_ref[...] + y_off + dskip_ref[:, gl] * xg
        v = y * _silu(z_ref[:, gl])
        v = v * lax.rsqrt(jnp.mean(v * v, axis=-1, keepdims=True) + EPS)
        y_ref[:, gl] = (v * g_ref[:, gl]).astype(y_ref.dtype)

    @pl.when(c == pl.num_programs(1) - 1)
    def _():
        st_out_ref[0] = st_ref[...].T


def _head_expand():
    return jnp.asarray(np.repeat(np.eye(SSD_HEADS, dtype=np.float32), SSD_HEADDIM, axis=1))


def _ssd_prompt(xbc, u_a, dt_raw, dt_bias, a_log, d_skip, g_ssd, batch, seq):
    m = xbc.shape[0]
    q = SSD_CHUNK
    nc = seq // q
    rowblk = lambda b, c: (b * nc + c, 0)
    const = lambda b, c: (0, 0)
    bc0 = SSD_INNER // (SSD_GROUPS * SSD_STATE)
    y, st = pl.pallas_call(
        _ssd_kernel,
        out_shape=(jax.ShapeDtypeStruct((m, SSD_INNER), BF16),
                   jax.ShapeDtypeStruct((batch, SSD_INNER, SSD_STATE), F32)),
        grid=(batch, nc),
        in_specs=[pl.BlockSpec((q, SSD_INNER), rowblk),
                  pl.BlockSpec((q, SSD_GROUPS * SSD_STATE), lambda b, c: (b * nc + c, bc0)),
                  pl.BlockSpec((q, SSD_GROUPS * SSD_STATE), lambda b, c: (b * nc + c, bc0 + 1)),
                  pl.BlockSpec((q, SSD_INNER), rowblk),
                  pl.BlockSpec((q, SSD_HEADS), rowblk),
                  pl.BlockSpec((SSD_HEADS, q), lambda b, c: (0, b * nc + c)),
                  pl.BlockSpec((1, SSD_HEADS), const), pl.BlockSpec((1, SSD_HEADS), const),
                  pl.BlockSpec((SSD_HEADS, 1), const), pl.BlockSpec((SSD_HEADS, 1), const),
                  pl.BlockSpec((1, SSD_INNER), const), pl.BlockSpec((1, SSD_INNER), const),
                  pl.BlockSpec((SSD_HEADS, SSD_INNER), const)],
        out_specs=(pl.BlockSpec((q, SSD_INNER), rowblk),
                   pl.BlockSpec((1, SSD_INNER, SSD_STATE), lambda b, c: (b, 0, 0))),
        scratch_shapes=[pltpu.VMEM((SSD_STATE, SSD_INNER), F32), pltpu.VMEM((q, SSD_GW), F32)],
        compiler_params=_cparams("parallel", "arbitrary"),
        name="ssd_prompt",
    )(xbc, xbc, xbc, u_a, dt_raw, dt_raw.T, dt_bias.reshape(1, -1), a_log.reshape(1, -1),
      dt_bias.reshape(-1, 1), a_log.reshape(-1, 1), jnp.repeat(d_skip, SSD_HEADDIM).reshape(1, -1),
      g_ssd.reshape(1, -1), _head_expand())
    return y, st.reshape(batch, SSD_HEADS, SSD_HEADDIM, SSD_STATE)


def _attn_kernel(q_ref, k_ref, v_ref, z_ref, o_ref, *, tq):
    qi = pl.program_id(2)
    q = q_ref[...]
    scale = 1.0 / math.sqrt(QK_NOPE + QK_ROPE)
    row = qi * tq + lax.broadcasted_iota(jnp.int32, (tq, tq), 0)
    col = lax.broadcasted_iota(jnp.int32, (tq, tq), 1)

    def body(j, carry):
        m, l, acc = carry
        off = pl.multiple_of(j * tq, tq)
        s = _dot_nt(q, k_ref[pl.ds(off, tq), :]) * scale
        s = jnp.where(col + off <= row, s, -jnp.inf)
        m_new = jnp.maximum(m, jnp.max(s, axis=-1, keepdims=True))
        p = jnp.exp(s - m_new)
        alpha = jnp.exp(m - m_new)
        l = alpha * l + jnp.sum(p, axis=-1, keepdims=True)
        acc = alpha * acc + jnp.dot(p.astype(BF16), v_ref[pl.ds(off, tq), :], preferred_element_type=F32)
        return m_new, l, acc

    init = (jnp.full((tq, 1), -jnp.inf, F32), jnp.zeros((tq, 1), F32), jnp.zeros((tq, V_DIM), F32))
    _, l, acc = lax.fori_loop(0, qi + 1, body, init)
    o_ref[...] = (acc / l * _silu(z_ref[...])).astype(o_ref.dtype)


def _attn_prompt(q, kcat, v, u_a, batch, seq):
    m = q.shape[0]
    tq = min(seq, 512)
    nq = seq // tq
    z0 = ZM_COL0 // V_DIM
    return pl.pallas_call(
        functools.partial(_attn_kernel, tq=tq),
        out_shape=jax.ShapeDtypeStruct((m, MLA_WIDTH), BF16),
        grid=(batch, MLA_HEADS, nq),
        in_specs=[pl.BlockSpec((tq, QK_PAD), lambda b, h, i: (b * nq + i, h)),
                  pl.BlockSpec((seq, QK_PAD), lambda b, h, i: (b, h)),
                  pl.BlockSpec((seq, V_DIM), lambda b, h, i: (b, h)),
                  pl.BlockSpec((tq, V_DIM), lambda b, h, i: (b * nq + i, z0 + h))],
        out_specs=pl.BlockSpec((tq, V_DIM), lambda b, h, i: (b * nq + i, h)),
        compiler_params=_cparams("parallel", "parallel", "arbitrary"),
        name="attn_prompt",
    )(q, kcat, v, u_a)


def _ple_kernel(ge_ref, p_ref, wu_ref, wp_ref, x_ref, o_ref):
    gate = _sigmoid(jnp.dot(ge_ref[...], wu_ref[...], preferred_element_type=F32))
    emb = jnp.dot(p_ref[...].astype(BF16), wp_ref[...], preferred_element_type=F32)
    o_ref[...] = x_ref[...] + gate * emb


def _ple(ge, p, w_up, w_ple, x):
    m, n = x.shape
    tm = min(m, 512)
    tn = 1024
    r = ge.shape[1]
    return pl.pallas_call(
        _ple_kernel,
        out_shape=jax.ShapeDtypeStruct((m, n), F32),
        grid=(m // tm, n // tn),
        in_specs=[pl.BlockSpec((tm, r), lambda i, j: (i, 0)),
                  pl.BlockSpec((tm, PLE_DIM), lambda i, j: (i, 0)),
                  pl.BlockSpec((r, tn), lambda i, j: (0, j)),
                  pl.BlockSpec((PLE_DIM, tn), lambda i, j: (0, j)),
                  pl.BlockSpec((tm, tn), lambda i, j: (i, j))],
        out_specs=pl.BlockSpec((tm, tn), lambda i, j: (i, j)),
        compiler_params=_cparams("parallel", "arbitrary"),
        name="ple",
    )(ge, p, w_up, w_ple, x)


def _ssd_step_kernel(xbc_ref, cst_ref, st_ref, zt_ref, dt_ref, w_ref, cb_ref, bias_ref, alog_ref,
                     dskip_ref, gt_ref, grp_ref, conv_out_ref, st_out_ref, y_ref):
    w = w_ref[...]
    x_new = xbc_ref[0]
    old = cst_ref[0]
    y = cb_ref[...] + x_new * w[CONV_W - 1:CONV_W, :]
    for k in range(CONV_W - 1):
        y = y + old[k:k + 1, :] * w[k:k + 1, :]
    for k in range(CONV_W - 2):
        conv_out_ref[0, k:k + 1, :] = old[k + 1:k + 2, :]
    conv_out_ref[0, CONV_W - 2:CONV_W - 1, :] = x_new
    xbc = _silu(y)

    dt = _softplus(dt_ref[0] + bias_ref[...])
    da = jnp.exp(dt * -jnp.exp(alog_ref[...]))
    hp = SSD_HEADDIM
    eye = (lax.broadcasted_iota(jnp.int32, (hp, hp), 0) == lax.broadcasted_iota(jnp.int32, (hp, hp), 1))
    lane_h = lax.broadcasted_iota(jnp.int32, (hp, SSD_HEADS), 1)
    xt = jnp.zeros((hp, SSD_HEADS), F32)
    yt = jnp.zeros((hp, SSD_HEADS), F32)
    for h in range(SSD_HEADS):
        g = h // SSD_HPG
        x_row = xbc[:, h * hp:(h + 1) * hp]
        x_col = jnp.sum(jnp.where(eye, x_row, 0.0), axis=-1, keepdims=True)
        b_row = xbc[:, SSD_INNER + g * SSD_STATE:SSD_INNER + (g + 1) * SSD_STATE]
        c_row = xbc[:, SSD_INNER + (SSD_GROUPS + g) * SSD_STATE:SSD_INNER + (SSD_GROUPS + g + 1) * SSD_STATE]
        st = st_ref[0, h * hp:(h + 1) * hp, :]
        new = st * da[:, h:h + 1] + (x_col * dt[:, h:h + 1]) * b_row
        st_out_ref[0, h * hp:(h + 1) * hp, :] = new
        y_col = jnp.sum(new * c_row, axis=-1, keepdims=True)
        xt = jnp.where(lane_h == h, x_col, xt)
        yt = jnp.where(lane_h == h, y_col, yt)
    yt = yt + dskip_ref[...] * xt
    v = yt * _silu(zt_ref[0])
    ss = jnp.sum(v * v, axis=0, keepdims=True)
    ss = jnp.dot(jnp.broadcast_to(ss, (SUBLANES, SSD_HEADS)), grp_ref[...], precision=HIGHEST,
                 preferred_element_type=F32)[0:1, :]
    v = v * lax.rsqrt(ss / (SSD_INNER // SSD_GROUPS) + EPS)
    y_ref[0] = (v * gt_ref[...]).astype(y_ref.dtype)


def _ssd_step(xbc_new, conv_state, ssm_state, z_s, dt_raw, conv_w, conv_b, dt_bias, a_log, d_skip, g_ssd):
    nb = xbc_new.shape[0]
    hp = SSD_HEADDIM
    to_t = lambda a: a.reshape(-1, SSD_HEADS, hp).swapaxes(1, 2)
    grp = jnp.asarray(np.kron(np.eye(SSD_GROUPS, dtype=np.float32), np.ones((SSD_HPG, SSD_HPG), np.float32)))
    const2 = lambda b: (0, 0)
    blk3 = lambda b: (b, 0, 0)
    conv_new, st_new, yt = pl.pallas_call(
        _ssd_step_kernel,
        out_shape=(jax.ShapeDtypeStruct((nb, CONV_W - 1, CONV_DIM), F32),
                   jax.ShapeDtypeStruct((nb, SSD_INNER, SSD_STATE), F32),
                   jax.ShapeDtypeStruct((nb, hp, SSD_HEADS), BF16)),
        grid=(nb,),
        in_specs=[pl.BlockSpec((1, 1, CONV_DIM), blk3),
                  pl.BlockSpec((1, CONV_W - 1, CONV_DIM), blk3),
                  pl.BlockSpec((1, SSD_INNER, SSD_STATE), blk3),
                  pl.BlockSpec((1, hp, SSD_HEADS), blk3),
                  pl.BlockSpec((1, 1, SSD_HEADS), blk3),
                  pl.BlockSpec((CONV_W, CONV_DIM), const2), pl.BlockSpec((1, CONV_DIM), const2),
                  pl.BlockSpec((1, SSD_HEADS), const2), pl.BlockSpec((1, SSD_HEADS), const2),
                  pl.BlockSpec((1, SSD_HEADS), const2), pl.BlockSpec((hp, SSD_HEADS), const2),
                  pl.BlockSpec((SSD_HEADS, SSD_HEADS), const2)],
        out_specs=(pl.BlockSpec((1, CONV_W - 1, CONV_DIM), blk3),
                   pl.BlockSpec((1, SSD_INNER, SSD_STATE), blk3),
                   pl.BlockSpec((1, hp, SSD_HEADS), blk3)),
        compiler_params=_cparams("parallel"),
        name="ssd_step",
    )(xbc_new.reshape(nb, 1, CONV_DIM), conv_state, ssm_state.reshape(nb, SSD_INNER, SSD_STATE),
      to_t(z_s), dt_raw.reshape(nb, 1, SSD_HEADS), conv_w, conv_b.reshape(1, -1),
      dt_bias.reshape(1, -1), a_log.reshape(1, -1), d_skip.reshape(1, -1), to_t(g_ssd)[0], grp)
    y = yt.swapaxes(1, 2).reshape(nb, SSD_INNER)
    return y, st_new.reshape(nb, SSD_HEADS, hp, SSD_STATE), conv_new


def _qlat_kernel(q_ref, w_ref, o_ref):
    q = q_ref[...]
    o_ref[:, :KV_LORA] = _dot_nt(q[:, :QK_NOPE], w_ref[...]).astype(o_ref.dtype)
    o_ref[:, KV_LORA:] = q[:, QK_NOPE:]


def _qlat(q, w_uk):
    nb = q.shape[0]
    out = pl.pallas_call(
        _qlat_kernel,
        out_shape=jax.ShapeDtypeStruct((nb, MLA_HEADS * QLAT_W), BF16),
        grid=(MLA_HEADS,),
        in_specs=[pl.BlockSpec((nb, QK_PAD), lambda h: (0, h)),
                  pl.BlockSpec((KV_LORA, QK_NOPE), lambda h: (0, h))],
        out_specs=pl.BlockSpec((nb, QLAT_W), lambda h: (0, h)),
        compiler_params=_cparams("parallel"),
        name="qlat",
    )(q, w_uk)
    return out.reshape(nb, MLA_HEADS, QLAT_W)


def _decode_attn_kernel(pt_ref, q_ref, cn_ref, kn_ref, *refs):
    npg = PAGES_PER_STEP
    ckv_refs, kr_refs = refs[:npg], refs[npg:2 * npg]
    o_ref, m_ref, l_ref, acc_ref = refs[2 * npg:]
    s_idx = pl.program_id(1)
    scale = 1.0 / math.sqrt(QK_NOPE + QK_ROPE)

    @pl.when(s_idx == 0)
    def _():
        m_ref[...] = jnp.full_like(m_ref, -jnp.inf)
        l_ref[...] = jnp.zeros_like(l_ref)
        acc_ref[...] = jnp.zeros_like(acc_ref)

    q = q_ref[0]
    q_lat = q[:, :KV_LORA]
    q_rope = q[:, KV_LORA:KV_LORA + QK_ROPE]
    pages = [r[...].astype(BF16) for r in ckv_refs]
    s = jnp.concatenate(
        [_dot_nt(q_lat, pg) + _dot_nt(q_rope, kr[...].astype(BF16)) for pg, kr in zip(pages, kr_refs)],
        axis=-1) * scale
    m_old = m_ref[...]
    m_new = jnp.maximum(m_old, jnp.max(s, axis=-1, keepdims=True))
    p = jnp.exp(s - m_new)
    alpha = jnp.exp(m_old - m_new)
    l_ref[...] = alpha * l_ref[...] + jnp.sum(p, axis=-1, keepdims=True)
    pb = p.astype(BF16)
    psz = pages[0].shape[0]
    pv = jnp.dot(pb[:, :psz], pages[0], preferred_element_type=F32)
    for k in range(1, npg):
        pv = pv + jnp.dot(pb[:, k * psz:(k + 1) * psz], pages[k], preferred_element_type=F32)
    acc_ref[...] = alpha * acc_ref[...] + pv
    m_ref[...] = m_new

    @pl.when(s_idx == pl.num_programs(1) - 1)
    def _():
        cn = cn_ref[0].astype(F32)
        kn = kn_ref[0].astype(F32)
        s_new = (jnp.sum(q_lat.astype(F32) * cn, axis=-1, keepdims=True)
                 + jnp.sum(q[:, KV_LORA:].astype(F32) * kn, axis=-1, keepdims=True)) * scale
        m_old = m_ref[...]
        m_fin = jnp.maximum(m_old, s_new)
        alpha = jnp.exp(m_old - m_fin)
        p_new = jnp.exp(s_new - m_fin)
        l = alpha * l_ref[...] + p_new
        acc = alpha * acc_ref[...] + p_new.astype(BF16).astype(F32) * cn
        o_ref[0] = (acc / l).astype(o_ref.dtype)


def _decode_attn(qlat, ckv_new_b, krp_new, cache_ckv, cache_krope, page_table, layer):
    nb, n_pages = page_table.shape
    psz = cache_ckv.shape[2]
    npg = PAGES_PER_STEP
    assert n_pages % npg == 0
    page_spec = lambda width, k: pl.BlockSpec(
        (None, None, psz, width), lambda b, s, pt: (layer, pt[b, s * npg + k], 0, 0))
    grid_spec = pltpu.PrefetchScalarGridSpec(
        num_scalar_prefetch=1,
        grid=(nb, n_pages // npg),
        in_specs=[pl.BlockSpec((1, MLA_HEADS, QLAT_W), lambda b, s, pt: (b, 0, 0)),
                  pl.BlockSpec((1, 1, KV_LORA), lambda b, s, pt: (b, 0, 0)),
                  pl.BlockSpec((1, 1, LANES), lambda b, s, pt: (b, 0, 0))]
        + [page_spec(KV_LORA, k) for k in range(npg)]
        + [page_spec(QK_ROPE, k) for k in range(npg)],
        out_specs=pl.BlockSpec((1, MLA_HEADS, KV_LORA), lambda b, s, pt: (b, 0, 0)),
        scratch_shapes=[pltpu.VMEM((MLA_HEADS, 1), F32), pltpu.VMEM((MLA_HEADS, 1), F32),
                        pltpu.VMEM((MLA_HEADS, KV_LORA), F32)],
    )
    return pl.pallas_call(
        _decode_attn_kernel,
        out_shape=jax.ShapeDtypeStruct((nb, MLA_HEADS, KV_LORA), BF16),
        grid_spec=grid_spec,
        compiler_params=_cparams("parallel", "arbitrary"),
        name="decode_attn",
    )(page_table, qlat, ckv_new_b.reshape(nb, 1, KV_LORA), krp_new.reshape(nb, 1, LANES),
      *([cache_ckv] * npg), *([cache_krope] * npg))


def _olat_kernel(o_ref, w_ref, z_ref, out_ref):
    acc = jnp.dot(o_ref[...], w_ref[...], preferred_element_type=F32)
    out_ref[...] = (acc * _silu(z_ref[...])).astype(out_ref.dtype)


def _olat_proj(o_lat, w_uv, z_m):
    nb = o_lat.shape[0]
    return pl.pallas_call(
        _olat_kernel,
        out_shape=jax.ShapeDtypeStruct((nb, MLA_WIDTH), BF16),
        grid=(MLA_HEADS,),
        in_specs=[pl.BlockSpec((nb, KV_LORA), lambda h: (0, h)),
                  pl.BlockSpec((KV_LORA, V_DIM), lambda h: (0, h)),
                  pl.BlockSpec((nb, V_DIM), lambda h: (0, h))],
        out_specs=pl.BlockSpec((nb, V_DIM), lambda h: (0, h)),
        compiler_params=_cparams("parallel"),
        name="olat_proj",
    )(o_lat, w_uv, z_m)


def _rope_tables(pos):
    half = QK_ROPE // 2
    inv = ROPE_THETA ** (-jnp.arange(half, dtype=F32) / half)
    ang = pos.astype(F32)[:, None] * inv[None, :]
    zeros = jnp.zeros((pos.shape[0], LANES - QK_ROPE), F32)
    cos_t = jnp.concatenate([jnp.cos(ang), jnp.cos(ang), zeros], axis=-1)
    sin_t = jnp.concatenate([jnp.sin(ang), jnp.sin(ang), zeros], axis=-1)
    return cos_t, sin_t


def _prep_weights(w_in, w_q_b, w_kv_b, w_ssd_o, w_mla_o, w_out, w_ple, w_pg_down, w_pg_up):
    o = _IN_OFFS
    seg = lambda k: w_in[:, o[k]:o[k + 1]]
    w_a = jnp.concatenate([seg(0), seg(1), seg(3)], axis=1).astype(BF16)
    w_mid = jnp.concatenate([seg(4), seg(5), seg(6), seg(2)], axis=1).astype(BF16)
    w_g = jnp.concatenate([seg(7), seg(8)], axis=1).astype(BF16)
    wq = w_q_b.reshape(Q_LORA, MLA_HEADS, QK_NOPE + QK_ROPE)
    wq = jnp.pad(wq, ((0, 0), (0, 0), (0, QK_PAD - QK_NOPE - QK_ROPE)))
    wq = wq.reshape(Q_LORA, MLA_HEADS * QK_PAD).astype(BF16)
    wkv = w_kv_b.reshape(KV_LORA, MLA_HEADS, QK_NOPE + V_DIM)
    w_uk = wkv[..., :QK_NOPE].reshape(KV_LORA, MLA_HEADS * QK_NOPE).astype(BF16)
    w_uv = wkv[..., QK_NOPE:].reshape(KV_LORA, MLA_HEADS * V_DIM).astype(BF16)
    c = lambda w: w.astype(BF16)
    return dict(w_a=w_a, w_mid=w_mid, w_g=w_g, wq=wq, w_uk=w_uk, w_uv=w_uv, w_ssd_o=c(w_ssd_o),
                w_mla_o=c(w_mla_o), w_out=c(w_out), w_ple=c(w_ple), w_pg_down=c(w_pg_down),
                w_pg_up=c(w_pg_up))


def _in_proj(x, g_norm, wts, g_q, g_kv, cos_t, sin_t):
    h = _rmsnorm(x, g_norm)
    u_a = _mm(h, wts["w_a"])
    u_g = _mm(h, wts["w_g"])
    u_mid = _mm(h, wts["w_mid"], tm=min(x.shape[0], 512))
    cqn, ckv, ckv_b, kr, krp, dt_raw = _mid_post(u_mid, g_q, g_kv, cos_t, sin_t)
    q = _mm(cqn, wts["wq"], mode="rope_q", extras=(cos_t, sin_t), out_dtype=BF16,
            table_rows=cos_t.shape[0], tm=min(x.shape[0], cos_t.shape[0], 1024))
    return u_a, u_g, ckv, ckv_b, kr, krp, dt_raw, q


def _out_proj(x, y_ssd, o_gated, u_g, p_l, wts, g_ple):
    tn = 512
    ys = _mm(y_ssd, wts["w_ssd_o"], mode="sig_mul", extras=(u_g,), tn=tn)
    merged = _mm(o_gated, wts["w_mla_o"], mode="sig_mul_add", extras=(u_g, ys),
                 extra_col0=(D_MODEL // tn, 0), out_dtype=BF16, tn=tn)
    x = _mm(merged, wts["w_out"], mode="residual", extras=(x,), tn=tn)
    ge = _mm(_rmsnorm(x, g_ple), wts["w_pg_down"], mode="gelu", out_dtype=BF16)
    return _ple(ge, p_l, wts["w_pg_up"], wts["w_ple"], x)


def _layer_prompt(x, p_l, tables, lw, wts, batch, seq):
    (g_norm, conv_w, conv_b, dt_bias, a_log, d_skip, g_ssd, g_q, g_kv, g_ple) = lw
    u_a, u_g, ckv, ckv_b, kr, krp, dt_raw, q = _in_proj(x, g_norm, wts, g_q, g_kv, *tables)
    xbc = _conv_prompt(u_a, conv_w, conv_b, seq)
    y_ssd, ssm_new = _ssd_prompt(xbc, u_a, dt_raw, dt_bias, a_log, d_skip, g_ssd, batch, seq)
    kcat = _mm(ckv_b, wts["w_uk"], mode="kcat", extras=(krp,), out_dtype=BF16)
    v = _mm(ckv_b, wts["w_uv"], out_dtype=BF16)
    o_gated = _attn_prompt(q, kcat, v, u_a, batch, seq)
    x = _out_proj(x, y_ssd, o_gated, u_g, p_l, wts, g_ple)
    conv_new = u_a.reshape(batch, seq, -1)[:, seq - (CONV_W - 1):, SSD_INNER:SSD_INNER + CONV_DIM]
    return x, ckv, kr, ssm_new, conv_new


def _layer_sample(x, p_l, tables, lw, wts, conv_state, ssm_state, cache_ckv, cache_krope, page_table, layer):
    (g_norm, conv_w, conv_b, dt_bias, a_log, d_skip, g_ssd, g_q, g_kv, g_ple) = lw
    u_a, u_g, ckv, ckv_b, kr, krp, dt_raw, q = _in_proj(x, g_norm, wts, g_q, g_kv, *tables)
    y_ssd, ssm_new, conv_new = _ssd_step(
        u_a[:, SSD_INNER:SSD_INNER + CONV_DIM], conv_state, ssm_state, u_a[:, :SSD_INNER], dt_raw,
        conv_w, conv_b, dt_bias, a_log, d_skip, g_ssd)
    qlat = _qlat(q, wts["w_uk"])
    o_lat = _decode_attn(qlat, ckv_b, krp, cache_ckv, cache_krope, page_table, layer)
    o_gated = _olat_proj(o_lat.reshape(x.shape[0], -1), wts["w_uv"], u_a[:, ZM_COL0:])
    x = _out_proj(x, y_ssd, o_gated, u_g, p_l, wts, g_ple)
    return x, ckv, kr, ssm_new, conv_new


def kernel(x_prompt, x_sample, p_prompt, p_sample, cache_ckv, cache_krope, state_ssm, state_conv,
           page_table, g_norm, w_in, conv_w, conv_b, dt_bias, a_log, d_skip, g_ssd, w_ssd_o,
           g_q, w_q_b, g_kv, w_kv_b, w_mla_o, w_out, g_ple, w_ple, w_pg_down, w_pg_up, g_final):
    b_p, s_p, _ = x_prompt.shape
    b_s, s_s, _ = x_sample.shape
    assert s_s == 1
    depth = w_in.shape[0]
    past_len = page_table.shape[1] * cache_ckv.shape[2]
    tab_p = _rope_tables(jnp.arange(s_p))
    tab_s = _rope_tables(jnp.full((b_s,), past_len))
    hp = x_prompt.reshape(b_p * s_p, D_MODEL)
    hs = x_sample.reshape(b_s, D_MODEL)
    outs_p, outs_s = [], []
    for i in range(depth):
        wts = _prep_weights(w_in[i], w_q_b[i], w_kv_b[i], w_ssd_o[i], w_mla_o[i], w_out[i], w_ple[i],
                            w_pg_down[i], w_pg_up[i])
        lw = (g_norm[i], conv_w[i], conv_b[i], dt_bias[i], a_log[i], d_skip[i], g_ssd[i], g_q[i],
              g_kv[i], g_ple[i])
        hp, *o_p = _layer_prompt(hp, p_prompt[i].reshape(b_p * s_p, PLE_DIM), tab_p, lw, wts, b_p, s_p)
        hs, *o_s = _layer_sample(hs, p_sample[i].reshape(b_s, PLE_DIM), tab_s, lw, wts, state_conv[i],
                                 state_ssm[i], cache_ckv, cache_krope, page_table, i)
        outs_p.append(o_p)
        outs_s.append(o_s)
    y_prompt = _rmsnorm(hp, g_final, F32).reshape(b_p, s_p, D_MODEL)
    y_sample = _rmsnorm(hs, g_final, F32).reshape(b_s, s_s, D_MODEL)
    stack = lambda outs, k, shape: jnp.stack([o[k] for o in outs]).reshape((depth,) + shape)
    return (y_prompt, y_sample,
            stack(outs_p, 0, (b_p, s_p, KV_LORA)), stack(outs_p, 1, (b_p, s_p, QK_ROPE)),
            stack(outs_p, 2, (b_p, SSD_HEADS, SSD_HEADDIM, SSD_STATE)),
            stack(outs_p, 3, (b_p, CONV_W - 1, CONV_DIM)),
            stack(outs_s, 0, (b_s, s_s, KV_LORA)), stack(outs_s, 1, (b_s, s_s, QK_ROPE)),
            stack(outs_s, 2, (b_s, SSD_HEADS, SSD_HEADDIM, SSD_STATE)),
            stack(outs_s, 3, (b_s, CONV_W - 1, CONV_DIM)))
```

```python
import functools
import math

import jax
import jax.numpy as jnp
import numpy as np
from jax import lax
from jax.experimental import pallas as pl
from jax.experimental.pallas import tpu as pltpu

F32 = jnp.float32
BF16 = jnp.bfloat16
HIGHEST = lax.Precision.HIGHEST

D_MODEL = 4096
SSD_INNER = 4096
SSD_HEADDIM = 64
SSD_HEADS = 64
SSD_GROUPS = 8
SSD_STATE = 128
SSD_HPG = SSD_HEADS // SSD_GROUPS
SSD_GW = SSD_HPG * SSD_HEADDIM
CONV_W = 4
CONV_DIM = SSD_INNER + 2 * SSD_GROUPS * SSD_STATE
SSD_CHUNK = 128
MLA_HEADS = 32
QK_NOPE = 128
QK_ROPE = 64
V_DIM = 128
Q_LORA = 768
KV_LORA = 512
MLA_WIDTH = MLA_HEADS * V_DIM
ROPE_THETA = 10000.0
PLE_DIM = 256
EPS = 1e-6
LANES = 128
SUBLANES = 8
QK_PAD = 2 * LANES
MID_W = Q_LORA + KV_LORA + LANES
QLAT_W = KV_LORA + LANES
PAGES_PER_STEP = 32
ATTN_SCALE = 1.0 / math.sqrt(QK_NOPE + QK_ROPE)
VMEM_LIMIT = 56 * 1024 * 1024

_IN_SIZES = (SSD_INNER, CONV_DIM, SSD_HEADS, MLA_WIDTH, Q_LORA, KV_LORA, QK_ROPE, D_MODEL, D_MODEL)
_IN_OFFS = tuple(int(v) for v in np.cumsum((0,) + _IN_SIZES))
ZM_COL0 = SSD_INNER + CONV_DIM


def _cparams(*sem):
    return pltpu.CompilerParams(dimension_semantics=sem, vmem_limit_bytes=VMEM_LIMIT)


def _sigmoid(x):
    return 1.0 / (1.0 + jnp.exp(-x))


def _silu(x):
    return x * _sigmoid(x)


def _softplus(x):
    return jnp.maximum(x, 0.0) + jnp.log(1.0 + jnp.exp(-jnp.abs(x)))


def _rope128(x, cos_t, sin_t):
    lane = lax.broadcasted_iota(jnp.int32, x.shape, 1)
    rot = jnp.where(lane < QK_ROPE // 2, -pltpu.roll(x, LANES - QK_ROPE // 2, 1),
                    pltpu.roll(x, QK_ROPE // 2, 1))
    return x * cos_t + rot * sin_t


def _dot_nt(a, b):
    return lax.dot_general(a, b, (((1,), (1,)), ((), ())), preferred_element_type=F32)


def _rmsnorm_kernel(x_ref, g_ref, o_ref):
    x = x_ref[...]
    y = x * lax.rsqrt(jnp.mean(x * x, axis=-1, keepdims=True) + EPS)
    o_ref[...] = (y * g_ref[...]).astype(o_ref.dtype)


def _rmsnorm(x, g, out_dtype=BF16):
    m, k = x.shape
    tm = min(m, 256)
    return pl.pallas_call(
        _rmsnorm_kernel,
        out_shape=jax.ShapeDtypeStruct((m, k), out_dtype),
        grid=(m // tm,),
        in_specs=[pl.BlockSpec((tm, k), lambda i: (i, 0)),
                  pl.BlockSpec((1, k), lambda i: (0, 0))],
        out_specs=pl.BlockSpec((tm, k), lambda i: (i, 0)),
        compiler_params=_cparams("parallel"),
        name="rmsnorm",
    )(x, g.reshape(1, k))


def _mm_kernel(a_ref, w_ref, *refs, mode):
    o_ref = refs[-1]
    acc = jnp.dot(a_ref[...], w_ref[...], preferred_element_type=F32)
    if mode == "plain":
        out = acc
    elif mode == "gelu":
        out = jax.nn.gelu(acc)
    elif mode == "sig_mul":
        out = _sigmoid(refs[0][...]) * acc
    elif mode == "sig_mul_add":
        out = _sigmoid(refs[0][...]) * acc + refs[1][...]
    elif mode == "silu_mul":
        out = _silu(refs[0][...]) * acc
    elif mode == "residual":
        out = refs[0][...] + acc
    elif mode == "rope_q":
        cos_t, sin_t = refs[0][...], refs[1][...]
        acc = acc * ATTN_SCALE
        for c in range(acc.shape[1] // QK_PAD):
            lo = c * QK_PAD
            o_ref[:, lo:lo + LANES] = acc[:, lo:lo + LANES].astype(o_ref.dtype)
            o_ref[:, lo + LANES:lo + QK_PAD] = _rope128(
                acc[:, lo + LANES:lo + QK_PAD], cos_t, sin_t).astype(o_ref.dtype)
        return
    elif mode == "kcat":
        krp = refs[0][...]
        for c in range(acc.shape[1] // LANES):
            o_ref[:, c * QK_PAD:c * QK_PAD + LANES] = acc[:, c * LANES:(c + 1) * LANES].astype(o_ref.dtype)
            o_ref[:, c * QK_PAD + LANES:(c + 1) * QK_PAD] = krp
        return
    else:
        raise ValueError(mode)
    o_ref[...] = out.astype(o_ref.dtype)


def _mm(a, w, *, mode="plain", extras=(), extra_col0=(), out_dtype=F32, tm=None, tn=512, a_col0=0,
        table_rows=None):
    m = a.shape[0]
    k, n = w.shape
    if tm is None:
        tm = min(m, 1024)
    tn = min(tn, n)
    assert m % tm == 0 and a_col0 % k == 0
    grid = (m // tm, pl.cdiv(n, tn))
    a_blk = a_col0 // k
    in_specs = [pl.BlockSpec((tm, k), lambda i, j: (i, a_blk)),
                pl.BlockSpec((k, tn), lambda i, j: (0, j))]
    out_n, out_tn = n, tn
    if mode == "rope_q":
        assert tn % QK_PAD == 0 and table_rows % tm == 0
        nt = table_rows // tm
        in_specs += [pl.BlockSpec((tm, LANES), lambda i, j: (i % nt, 0))] * 2
    elif mode == "kcat":
        in_specs += [pl.BlockSpec((tm, LANES), lambda i, j: (i, 0))]
        out_n, out_tn = 2 * n, 2 * tn
    else:
        for c0 in (tuple(extra_col0) + (0,) * len(extras))[:len(extras)]:
            in_specs.append(pl.BlockSpec((tm, tn), functools.partial(lambda i, j, c: (i, j + c), c=c0)))
    return pl.pallas_call(
        functools.partial(_mm_kernel, mode=mode),
        out_shape=jax.ShapeDtypeStruct((m, out_n), out_dtype),
        grid=grid,
        in_specs=in_specs,
        out_specs=pl.BlockSpec((tm, out_tn), lambda i, j: (i, j)),
        compiler_params=_cparams("parallel", "arbitrary"),
        name="mm_" + mode,
    )(a, w, *extras)


def _mid_kernel(u_ref, gq_ref, gkv_ref, cos_ref, sin_ref,
                cqn_ref, ckv_ref, ckvb_ref, kr_ref, krp_ref, dt_ref):
    def norm(x, g):
        return x * lax.rsqrt(jnp.mean(x * x, axis=-1, keepdims=True) + EPS) * g

    cqn_ref[...] = norm(u_ref[:, :Q_LORA], gq_ref[...]).astype(cqn_ref.dtype)
    ckv = norm(u_ref[:, Q_LORA:Q_LORA + KV_LORA], gkv_ref[...])
    ckv_ref[...] = ckv
    ckvb_ref[...] = ckv.astype(ckvb_ref.dtype)
    krdt = u_ref[:, Q_LORA + KV_LORA:]
    lane = lax.broadcasted_iota(jnp.int32, krdt.shape, 1)
    kr = _rope128(jnp.where(lane < QK_ROPE, krdt, 0.0), cos_ref[...], sin_ref[...])
    kr_ref[...] = kr[:, :QK_ROPE]
    krp_ref[...] = kr.astype(krp_ref.dtype)
    dt_ref[...] = pltpu.roll(krdt, QK_ROPE, 1)[:, :SSD_HEADS]


def _mid_post(u_mid, g_q, g_kv, cos_t, sin_t):
    m = u_mid.shape[0]
    table_rows = cos_t.shape[0]
    tm = min(m, table_rows, 256)
    nt = table_rows // tm
    row = lambda i: (i, 0)
    return pl.pallas_call(
        _mid_kernel,
        out_shape=(jax.ShapeDtypeStruct((m, Q_LORA), BF16),
                   jax.ShapeDtypeStruct((m, KV_LORA), F32),
                   jax.ShapeDtypeStruct((m, KV_LORA), BF16),
                   jax.ShapeDtypeStruct((m, QK_ROPE), F32),
                   jax.ShapeDtypeStruct((m, LANES), BF16),
                   jax.ShapeDtypeStruct((m, SSD_HEADS), F32)),
        grid=(m // tm,),
        in_specs=[pl.BlockSpec((tm, MID_W), row),
                  pl.BlockSpec((1, Q_LORA), lambda i: (0, 0)),
                  pl.BlockSpec((1, KV_LORA), lambda i: (0, 0)),
                  pl.BlockSpec((tm, LANES), lambda i: (i % nt, 0)),
                  pl.BlockSpec((tm, LANES), lambda i: (i % nt, 0))],
        out_specs=(pl.BlockSpec((tm, Q_LORA), row), pl.BlockSpec((tm, KV_LORA), row),
                   pl.BlockSpec((tm, KV_LORA), row), pl.BlockSpec((tm, QK_ROPE), row),
                   pl.BlockSpec((tm, LANES), row), pl.BlockSpec((tm, SSD_HEADS), row)),
        compiler_params=_cparams("parallel"),
        name="mid_post",
    )(u_mid, g_q.reshape(1, -1), g_kv.reshape(1, -1), cos_t, sin_t)


def _conv_kernel(x_ref, halo_ref, w_ref, b_ref, o_ref, *, tiles_per_seq):
    i = pl.program_id(1)
    x = x_ref[...]
    halo = jnp.where(i % tiles_per_seq == 0, 0.0, halo_ref[...])
    ext = jnp.concatenate([halo, x], axis=0)
    w = w_ref[...]
    y = b_ref[...] + x * w[CONV_W - 1:CONV_W, :]
    for s in range(1, CONV_W):
        y = y + pltpu.roll(ext, s, 0)[SUBLANES:, :] * w[CONV_W - 1 - s:CONV_W - s, :]
    o_ref[...] = _silu(y).astype(o_ref.dtype)


def _conv_prompt(u_a, conv_w, conv_b, seq):
    m = u_a.shape[0]
    tt = min(seq, 512)
    tc = 512
    c0 = SSD_INNER // tc
    hb = tt // SUBLANES
    return pl.pallas_call(
        functools.partial(_conv_kernel, tiles_per_seq=seq // tt),
        out_shape=jax.ShapeDtypeStruct((m, CONV_DIM), F32),
        grid=(CONV_DIM // tc, m // tt),
        in_specs=[pl.BlockSpec((tt, tc), lambda j, i: (i, c0 + j)),
                  pl.BlockSpec((SUBLANES, tc), lambda j, i: (jnp.maximum(i * hb - 1, 0), c0 + j)),
                  pl.BlockSpec((CONV_W, tc), lambda j, i: (0, j)),
                  pl.BlockSpec((1, tc), lambda j, i: (0, j))],
        out_specs=pl.BlockSpec((tt, tc), lambda j, i: (i, j)),
        compiler_params=_cparams("parallel", "arbitrary"),
        name="conv_prompt",
    )(u_a, u_a, conv_w, conv_b.reshape(1, -1))


def _ssd_kernel(x_ref, b_ref, c_ref, z_ref, dt_ref, dtt_ref, bias_ref, alog_ref, biast_ref, alogt_ref,
                dskip_ref, g_ref, e_ref, y_ref, st_out_ref, st_ref, yd_ref):
    c = pl.program_id(1)
    q = SSD_CHUNK

    @pl.when(c == 0)
    def _():
        st_ref[...] = jnp.zeros_like(st_ref)

    dt = _softplus(dt_ref[...] + bias_ref[...])
    dta = dt * -jnp.exp(alog_ref[...])
    dtt = _softplus(dtt_ref[...] + biast_ref[...])
    dtat = dtt * -jnp.exp(alogt_ref[...])
    row = lax.broadcasted_iota(jnp.int32, (q, q), 0)
    col = lax.broadcasted_iota(jnp.int32, (q, q), 1)
    tri = row >= col
    cum = jnp.dot(tri.astype(F32), dta, precision=HIGHEST, preferred_element_type=F32)
    cumt = jnp.dot(dtat, (row <= col).astype(F32), precision=HIGHEST, preferred_element_type=F32)
    cum_last = cum[q - 1:q, :]
    fac = jnp.concatenate([jnp.exp(cum), jnp.exp(cum_last - cum) * dt,
                           jnp.broadcast_to(jnp.exp(cum_last), (SUBLANES, SSD_HEADS))], axis=0)
    fac_hi = fac.astype(BF16)
    fac_lo = (fac - fac_hi.astype(F32)).astype(BF16)
    e = e_ref[...]
    fac = (jnp.dot(fac_hi, e, preferred_element_type=F32) + jnp.dot(fac_lo, e, preferred_element_type=F32))
    e_cum = fac[:q]
    dw = fac[q:2 * q]
    dec = fac[2 * q:2 * q + 1]

    for g in range(SSD_GROUPS):
        gl = slice(g * SSD_GW, (g + 1) * SSD_GW)
        nl = slice(g * SSD_STATE, (g + 1) * SSD_STATE)
        bg = b_ref[:, nl]
        cg = c_ref[:, nl].astype(BF16)
        cb = _dot_nt(cg, bg.astype(BF16))
        xg = x_ref[:, gl]
        xg_b = xg.astype(BF16)
        st_g = st_ref[:, gl]
        y_off = jnp.dot(cg, st_g.astype(BF16), preferred_element_type=F32) * e_cum[:, gl]
        for r in range(SSD_HPG):
            h = g * SSD_HPG + r
            lmat = jnp.exp(jnp.where(tri, cum[:, h:h + 1] - cumt[h:h + 1, :], -jnp.inf))
            mm = (cb * lmat * dtt[h:h + 1, :]).astype(BF16)
            yd_ref[:, r * SSD_HEADDIM:(r + 1) * SSD_HEADDIM] = jnp.dot(
                mm, xg_b[:, r * SSD_HEADDIM:(r + 1) * SSD_HEADDIM], preferred_element_type=F32)
        xw = (xg * dw[:, gl]).astype(BF16)
        st_ref[:, gl] = st_g * dec[:, gl] + jnp.dot(bg.T.astype(BF16), xw, preferred_element_type=F32)
        y = yd_ref[...] + y_off + dskip_ref[:, gl] * xg
        v = y * _silu(z_ref[:, gl])
        v = v * lax.rsqrt(jnp.mean(v * v, axis=-1, keepdims=True) + EPS)
        y_ref[:, gl] = (v * g_ref[:, gl]).astype(y_ref.dtype)

    @pl.when(c == pl.num_programs(1) - 1)
    def _():
        st_out_ref[0] = st_ref[...].T


def _head_expand():
    return jnp.asarray(np.repeat(np.eye(SSD_HEADS, dtype=np.float32), SSD_HEADDIM, axis=1), dtype=BF16)


def _ssd_prompt(xbc, u_a, dt_raw, dt_bias, a_log, d_skip, g_ssd, batch, seq):
    m = xbc.shape[0]
    q = SSD_CHUNK
    nc = seq // q
    rowblk = lambda b, c: (b * nc + c, 0)
    const = lambda b, c: (0, 0)
    bc0 = SSD_INNER // (SSD_GROUPS * SSD_STATE)
    y, st = pl.pallas_call(
        _ssd_kernel,
        out_shape=(jax.ShapeDtypeStruct((m, SSD_INNER), BF16),
                   jax.ShapeDtypeStruct((batch, SSD_INNER, SSD_STATE), F32)),
        grid=(batch, nc),
        in_specs=[pl.BlockSpec((q, SSD_INNER), rowblk),
                  pl.BlockSpec((q, SSD_GROUPS * SSD_STATE), lambda b, c: (b * nc + c, bc0)),
                  pl.BlockSpec((q, SSD_GROUPS * SSD_STATE), lambda b, c: (b * nc + c, bc0 + 1)),
                  pl.BlockSpec((q, SSD_INNER), rowblk),
                  pl.BlockSpec((q, SSD_HEADS), rowblk),
                  pl.BlockSpec((SSD_HEADS, q), lambda b, c: (0, b * nc + c)),
                  pl.BlockSpec((1, SSD_HEADS), const), pl.BlockSpec((1, SSD_HEADS), const),
                  pl.BlockSpec((SSD_HEADS, 1), const), pl.BlockSpec((SSD_HEADS, 1), const),
                  pl.BlockSpec((1, SSD_INNER), const), pl.BlockSpec((1, SSD_INNER), const),
                  pl.BlockSpec((SSD_HEADS, SSD_INNER), const)],
        out_specs=(pl.BlockSpec((q, SSD_INNER), rowblk),
                   pl.BlockSpec((1, SSD_INNER, SSD_STATE), lambda b, c: (b, 0, 0))),
        scratch_shapes=[pltpu.VMEM((SSD_STATE, SSD_INNER), F32), pltpu.VMEM((q, SSD_GW), F32)],
        compiler_params=_cparams("parallel", "arbitrary"),
        name="ssd_prompt",
    )(xbc, xbc, xbc, u_a, dt_raw, dt_raw.T, dt_bias.reshape(1, -1), a_log.reshape(1, -1),
      dt_bias.reshape(-1, 1), a_log.reshape(-1, 1), jnp.repeat(d_skip, SSD_HEADDIM).reshape(1, -1),
      g_ssd.reshape(1, -1), _head_expand())
    return y, st.reshape(batch, SSD_HEADS, SSD_HEADDIM, SSD_STATE)


def _attn_kernel(q_ref, k_ref, v_ref, z_ref, o_ref, *, tq):
    qi = pl.program_id(2)
    q = q_ref[...]

    def step(j, carry, diagonal):
        m, l, acc = carry
        off = pl.multiple_of(j * tq, tq)
        s = _dot_nt(q, k_ref[pl.ds(off, tq), :])
        if diagonal:
            row = lax.broadcasted_iota(jnp.int32, (tq, tq), 0)
            col = lax.broadcasted_iota(jnp.int32, (tq, tq), 1)
            s = jnp.where(col <= row, s, -jnp.inf)
        m_new = jnp.maximum(m, jnp.max(s, axis=-1, keepdims=True))
        p = jnp.exp(s - m_new)
        alpha = jnp.exp(m - m_new)
        l = alpha * l + jnp.sum(p, axis=-1, keepdims=True)
        acc = alpha * acc + jnp.dot(p.astype(BF16), v_ref[pl.ds(off, tq), :], preferred_element_type=F32)
        return m_new, l, acc

    init = (jnp.full((tq, 1), -jnp.inf, F32), jnp.zeros((tq, 1), F32), jnp.zeros((tq, V_DIM), F32))
    carry = lax.fori_loop(0, qi, functools.partial(step, diagonal=False), init)
    _, l, acc = step(qi, carry, True)
    o_ref[...] = (acc / l * _silu(z_ref[...])).astype(o_ref.dtype)


def _attn_prompt(q, kcat, v, u_a, batch, seq):
    m = q.shape[0]
    tq = min(seq, 512)
    nq = seq // tq
    z0 = ZM_COL0 // V_DIM
    return pl.pallas_call(
        functools.partial(_attn_kernel, tq=tq),
        out_shape=jax.ShapeDtypeStruct((m, MLA_WIDTH), BF16),
        grid=(batch, MLA_HEADS, nq),
        in_specs=[pl.BlockSpec((tq, QK_PAD), lambda b, h, i: (b * nq + i, h)),
                  pl.BlockSpec((seq, QK_PAD), lambda b, h, i: (b, h)),
                  pl.BlockSpec((seq, V_DIM), lambda b, h, i: (b, h)),
                  pl.BlockSpec((tq, V_DIM), lambda b, h, i: (b * nq + i, z0 + h))],
        out_specs=pl.BlockSpec((tq, V_DIM), lambda b, h, i: (b * nq + i, h)),
        compiler_params=_cparams("parallel", "parallel", "arbitrary"),
        name="attn_prompt",
    )(q, kcat, v, u_a)


def _ple_kernel(ge_ref, p_ref, wu_ref, wp_ref, x_ref, o_ref):
    gate = _sigmoid(jnp.dot(ge_ref[...], wu_ref[...], preferred_element_type=F32))
    emb = jnp.dot(p_ref[...].astype(BF16), wp_ref[...], preferred_element_type=F32)
    o_ref[...] = x_ref[...] + gate * emb


def _ple(ge, p, w_up, w_ple, x):
    m, n = x.shape
    tm = min(m, 512)
    tn = 1024
    r = ge.shape[1]
    return pl.pallas_call(
        _ple_kernel,
        out_shape=jax.ShapeDtypeStruct((m, n), F32),
        grid=(m // tm, n // tn),
        in_specs=[pl.BlockSpec((tm, r), lambda i, j: (i, 0)),
                  pl.BlockSpec((tm, PLE_DIM), lambda i, j: (i, 0)),
                  pl.BlockSpec((r, tn), lambda i, j: (0, j)),
                  pl.BlockSpec((PLE_DIM, tn), lambda i, j: (0, j)),
                  pl.BlockSpec((tm, tn), lambda i, j: (i, j))],
        out_specs=pl.BlockSpec((tm, tn), lambda i, j: (i, j)),
        compiler_params=_cparams("parallel", "arbitrary"),
        name="ple",
    )(ge, p, w_up, w_ple, x)


def _ssd_step_kernel(xbc_ref, cst_ref, xt_ref, cstt_ref, st_ref, zt_ref, dt_ref, dtx_ref, w_ref, cb_ref,
                     wt_ref, cbt_ref, bias_ref, biasx_ref, alog_ref, dskip_ref, gt_ref, grp_ref, *refs):
    conv_out_ref, st_out_ref, y_ref = refs[-3:]
    hp = SSD_HEADDIM
    w = w_ref[...]
    x_new = xbc_ref[0]
    old = cst_ref[0]
    y = cb_ref[...] + x_new * w[CONV_W - 1:CONV_W, :]
    for k in range(CONV_W - 1):
        y = y + old[k:k + 1, :] * w[k:k + 1, :]
    for k in range(CONV_W - 2):
        conv_out_ref[0, k:k + 1, :] = old[k + 1:k + 2, :]
    conv_out_ref[0, CONV_W - 2:CONV_W - 1, :] = x_new
    xbc = _silu(y)
    xdt = xbc[:, :SSD_INNER] * _softplus(dtx_ref[0] + biasx_ref[...])
    wt = wt_ref[...]
    xt = cbt_ref[...] + xt_ref[0] * wt[CONV_W - 1]
    for k in range(CONV_W - 1):
        xt = xt + cstt_ref[0, k] * wt[k]
    xt = _silu(xt)

    dt = _softplus(dt_ref[0] + bias_ref[...])
    da = jnp.exp(dt * -jnp.exp(alog_ref[...]))
    eye_h = (lax.broadcasted_iota(jnp.int32, (SSD_HEADS, SSD_HEADS), 0)
             == lax.broadcasted_iota(jnp.int32, (SSD_HEADS, SSD_HEADS), 1)).astype(F32)
    dab = lax.dot_general(eye_h, jnp.broadcast_to(da, (SSD_STATE, SSD_HEADS)), (((1,), (1,)), ((), ())),
                          precision=HIGHEST, preferred_element_type=F32)
    rowi = lax.broadcasted_iota(jnp.int32, (hp, LANES), 0)
    lanei = lax.broadcasted_iota(jnp.int32, (hp, LANES), 1)
    diag = (rowi == lanei, rowi + hp == lanei)
    lane_h = lax.broadcasted_iota(jnp.int32, (hp, SSD_HEADS), 1)
    yt = jnp.zeros((hp, SSD_HEADS), F32)
    for g in range(SSD_GROUPS):
        b0 = SSD_INNER + g * SSD_STATE
        c0 = SSD_INNER + (SSD_GROUPS + g) * SSD_STATE
        b_mat = jnp.broadcast_to(xbc[:, b0:b0 + SSD_STATE], (LANES, SSD_STATE)).astype(BF16)
        c_mat = jnp.broadcast_to(xbc[:, c0:c0 + SSD_STATE], (LANES, SSD_STATE)).astype(BF16)
        a_parts, da_parts = [], []
        for r in range(SSD_HPG):
            h = g * SSD_HPG + r
            pair = xdt[:, (h // 2) * LANES:(h // 2 + 1) * LANES]
            a_parts.append(jnp.where(diag[h % 2], pair, 0.0).astype(BF16))
            da_parts.append(jnp.broadcast_to(dab[h:h + 1, :], (hp, SSD_STATE)))
        rows = slice(g * SSD_GW, (g + 1) * SSD_GW)
        inc = jnp.dot(jnp.concatenate(a_parts, axis=0), b_mat, preferred_element_type=F32)
        new = st_ref[0, rows, :] * jnp.concatenate(da_parts, axis=0) + inc
        st_out_ref[0, rows, :] = new
        yb = _dot_nt(new.astype(BF16), c_mat)
        for r in range(SSD_HPG):
            h = g * SSD_HPG + r
            yt = jnp.where(lane_h == h, yb[r * hp:(r + 1) * hp, :SSD_HEADS], yt)
    yt = yt + dskip_ref[...] * xt
    v = yt * _silu(zt_ref[0])
    ss = jnp.sum(v * v, axis=0, keepdims=True)
    ss = jnp.dot(jnp.broadcast_to(ss, (SUBLANES, SSD_HEADS)), grp_ref[...], precision=HIGHEST,
                 preferred_element_type=F32)[0:1, :]
    v = v * lax.rsqrt(ss / (SSD_INNER // SSD_GROUPS) + EPS)
    y_ref[0] = (v * gt_ref[...]).astype(y_ref.dtype)


def _ssd_step(xbc_new, conv_state, ssm_states, ssm_stack, layer, z_s, dt_raw, conv_w, conv_b, dt_bias, a_log,
              d_skip, g_ssd):
    nb = xbc_new.shape[0]
    hp = SSD_HEADDIM
    to_t = lambda a: a.reshape(a.shape[:-1] + (SSD_HEADS, hp)).swapaxes(-1, -2)
    rep = lambda a: jnp.repeat(a, hp, axis=-1)
    grp = jnp.asarray(np.kron(np.eye(SSD_GROUPS, dtype=np.float32), np.ones((SSD_HPG, SSD_HPG), np.float32)))
    const2 = lambda b: (0, 0)
    const3 = lambda b: (0, 0, 0)
    blk3 = lambda b: (b, 0, 0)
    blk4 = lambda b: (b, 0, 0, 0)
    st_spec = pl.BlockSpec((None, 1, SSD_INNER, SSD_STATE), lambda b: (layer, b, 0, 0))
    in_specs = [pl.BlockSpec((1, 1, CONV_DIM), blk3),
                pl.BlockSpec((1, CONV_W - 1, CONV_DIM), blk3),
                pl.BlockSpec((1, hp, SSD_HEADS), blk3),
                pl.BlockSpec((1, CONV_W - 1, hp, SSD_HEADS), blk4),
                st_spec,
                pl.BlockSpec((1, hp, SSD_HEADS), blk3),
                pl.BlockSpec((1, 1, SSD_HEADS), blk3),
                pl.BlockSpec((1, 1, SSD_INNER), blk3),
                pl.BlockSpec((CONV_W, CONV_DIM), const2), pl.BlockSpec((1, CONV_DIM), const2),
                pl.BlockSpec((CONV_W, hp, SSD_HEADS), const3), pl.BlockSpec((hp, SSD_HEADS), const2),
                pl.BlockSpec((1, SSD_HEADS), const2), pl.BlockSpec((1, SSD_INNER), const2),
                pl.BlockSpec((1, SSD_HEADS), const2), pl.BlockSpec((1, SSD_HEADS), const2),
                pl.BlockSpec((hp, SSD_HEADS), const2), pl.BlockSpec((SSD_HEADS, SSD_HEADS), const2)]
    args = [xbc_new.reshape(nb, 1, CONV_DIM), conv_state, to_t(xbc_new[:, :SSD_INNER]),
            to_t(conv_state[:, :, :SSD_INNER]), ssm_states, to_t(z_s), dt_raw.reshape(nb, 1, SSD_HEADS),
            rep(dt_raw).reshape(nb, 1, SSD_INNER), conv_w, conv_b.reshape(1, -1), to_t(conv_w[:, :SSD_INNER]),
            to_t(conv_b[:SSD_INNER]), dt_bias.reshape(1, -1), rep(dt_bias).reshape(1, -1), a_log.reshape(1, -1),
            d_skip.reshape(1, -1), to_t(g_ssd), grp]
    aliases = {}
    if ssm_stack is not None:
        in_specs.append(pl.BlockSpec(memory_space=pl.ANY))
        args.append(ssm_stack)
        aliases = {len(args) - 1: 1}
    conv_new, ssm_stack, yt = pl.pallas_call(
        _ssd_step_kernel,
        out_shape=(jax.ShapeDtypeStruct((nb, CONV_W - 1, CONV_DIM), F32),
                   jax.ShapeDtypeStruct(ssm_states.shape, F32),
                   jax.ShapeDtypeStruct((nb, hp, SSD_HEADS), BF16)),
        grid=(nb,),
        in_specs=in_specs,
        out_specs=(pl.BlockSpec((1, CONV_W - 1, CONV_DIM), blk3), st_spec,
                   pl.BlockSpec((1, hp, SSD_HEADS), blk3)),
        input_output_aliases=aliases,
        compiler_params=_cparams("arbitrary"),
        name="ssd_step",
    )(*args)
    y = yt.swapaxes(1, 2).reshape(nb, SSD_INNER)
    return y, ssm_stack, conv_new


def _qlat_kernel(q_ref, w_ref, o_ref):
    q = q_ref[...]
    o_ref[:, :KV_LORA] = _dot_nt(q[:, :QK_NOPE], w_ref[...]).astype(o_ref.dtype)
    o_ref[:, KV_LORA:] = q[:, QK_NOPE:]


def _qlat(q, w_uk):
    nb = q.shape[0]
    out = pl.pallas_call(
        _qlat_kernel,
        out_shape=jax.ShapeDtypeStruct((nb, MLA_HEADS * QLAT_W), BF16),
        grid=(MLA_HEADS,),
        in_specs=[pl.BlockSpec((nb, QK_PAD), lambda h: (0, h)),
                  pl.BlockSpec((KV_LORA, QK_NOPE), lambda h: (0, h))],
        out_specs=pl.BlockSpec((nb, QLAT_W), lambda h: (0, h)),
        compiler_params=_cparams("parallel"),
        name="qlat",
    )(q, w_uk)
    return out.reshape(nb, MLA_HEADS, QLAT_W)


def _decode_attn_kernel(pt_ref, q_ref, cn_ref, kn_ref, *refs, npg):
    ckv_refs, kr_refs = refs[:npg], refs[npg:2 * npg]
    o_ref, m_ref, l_ref, acc_ref = refs[2 * npg:]
    s_idx = pl.program_id(1)

    @pl.when(s_idx == 0)
    def _():
        m_ref[...] = jnp.full_like(m_ref, -jnp.inf)
        l_ref[...] = jnp.zeros_like(l_ref)
        acc_ref[...] = jnp.zeros_like(acc_ref)

    q = q_ref[0]
    q_lat = q[:, :KV_LORA]
    q_rope = q[:, KV_LORA:KV_LORA + QK_ROPE]
    pages = [r[...].astype(BF16) for r in ckv_refs]
    s = jnp.concatenate(
        [_dot_nt(q_lat, pg) + jnp.dot(q_rope, kr[...].astype(BF16), preferred_element_type=F32)
         for pg, kr in zip(pages, kr_refs)], axis=-1)
    m_old = m_ref[...]
    m_new = jnp.maximum(m_old, jnp.max(s, axis=-1, keepdims=True))
    p = jnp.exp(s - m_new)
    alpha = jnp.exp(m_old - m_new)
    l_ref[...] = alpha * l_ref[...] + jnp.sum(p, axis=-1, keepdims=True)
    pb = p.astype(BF16)
    psz = pages[0].shape[0]
    pv = jnp.dot(pb[:, :psz], pages[0], preferred_element_type=F32)
    for k in range(1, npg):
        pv = pv + jnp.dot(pb[:, k * psz:(k + 1) * psz], pages[k], preferred_element_type=F32)
    acc_ref[...] = alpha * acc_ref[...] + pv
    m_ref[...] = m_new

    @pl.when(s_idx == pl.num_programs(1) - 1)
    def _():
        cn = cn_ref[0].astype(F32)
        kn = kn_ref[0].astype(F32)
        s_new = (jnp.sum(q_lat.astype(F32) * cn, axis=-1, keepdims=True)
                 + jnp.sum(q[:, KV_LORA:].astype(F32) * kn, axis=-1, keepdims=True))
        m_old = m_ref[...]
        m_fin = jnp.maximum(m_old, s_new)
        alpha = jnp.exp(m_old - m_fin)
        p_new = jnp.exp(s_new - m_fin)
        l = alpha * l_ref[...] + p_new
        acc = alpha * acc_ref[...] + p_new.astype(BF16).astype(F32) * cn
        o_ref[0] = (acc / l).astype(o_ref.dtype)


def _decode_attn(qlat, ckv_new_b, krp_new, cache_ckv, cache_krope_t, page_table, layer):
    nb, n_pages = page_table.shape
    psz = cache_ckv.shape[2]
    npg = min(PAGES_PER_STEP, n_pages)
    assert n_pages % npg == 0
    page_idx = lambda k: (lambda b, s, pt: (layer, pt[b, s * npg + k], 0, 0))
    grid_spec = pltpu.PrefetchScalarGridSpec(
        num_scalar_prefetch=1,
        grid=(nb, n_pages // npg),
        in_specs=[pl.BlockSpec((1, MLA_HEADS, QLAT_W), lambda b, s, pt: (b, 0, 0)),
                  pl.BlockSpec((1, 1, KV_LORA), lambda b, s, pt: (b, 0, 0)),
                  pl.BlockSpec((1, 1, LANES), lambda b, s, pt: (b, 0, 0))]
        + [pl.BlockSpec((None, None, psz, KV_LORA), page_idx(k)) for k in range(npg)]
        + [pl.BlockSpec((None, None, QK_ROPE, psz), page_idx(k)) for k in range(npg)],
        out_specs=pl.BlockSpec((1, MLA_HEADS, KV_LORA), lambda b, s, pt: (b, 0, 0)),
        scratch_shapes=[pltpu.VMEM((MLA_HEADS, 1), F32), pltpu.VMEM((MLA_HEADS, 1), F32),
                        pltpu.VMEM((MLA_HEADS, KV_LORA), F32)],
    )
    return pl.pallas_call(
        functools.partial(_decode_attn_kernel, npg=npg),
        out_shape=jax.ShapeDtypeStruct((nb, MLA_HEADS, KV_LORA), BF16),
        grid_spec=grid_spec,
        compiler_params=_cparams("parallel", "arbitrary"),
        name="decode_attn",
    )(page_table, qlat, ckv_new_b.reshape(nb, 1, KV_LORA), krp_new.reshape(nb, 1, LANES),
      *([cache_ckv] * npg), *([cache_krope_t] * npg))


def _olat_kernel(o_ref, w_ref, z_ref, out_ref):
    acc = jnp.dot(o_ref[...], w_ref[...], preferred_element_type=F32)
    out_ref[...] = (acc * _silu(z_ref[...])).astype(out_ref.dtype)


def _olat_proj(o_lat, w_uv, z_m):
    nb = o_lat.shape[0]
    return pl.pallas_call(
        _olat_kernel,
        out_shape=jax.ShapeDtypeStruct((nb, MLA_WIDTH), BF16),
        grid=(MLA_HEADS,),
        in_specs=[pl.BlockSpec((nb, KV_LORA), lambda h: (0, h)),
                  pl.BlockSpec((KV_LORA, V_DIM), lambda h: (0, h)),
                  pl.BlockSpec((nb, V_DIM), lambda h: (0, h))],
        out_specs=pl.BlockSpec((nb, V_DIM), lambda h: (0, h)),
        compiler_params=_cparams("parallel"),
        name="olat_proj",
    )(o_lat, w_uv, z_m)


def _rope_tables(pos):
    half = QK_ROPE // 2
    inv = ROPE_THETA ** (-jnp.arange(half, dtype=F32) / half)
    ang = pos.astype(F32)[:, None] * inv[None, :]
    zeros = jnp.zeros((pos.shape[0], LANES - QK_ROPE), F32)
    cos_t = jnp.concatenate([jnp.cos(ang), jnp.cos(ang), zeros], axis=-1)
    sin_t = jnp.concatenate([jnp.sin(ang), jnp.sin(ang), zeros], axis=-1)
    return cos_t, sin_t


def _prep_weights(w_in, w_q_b, w_kv_b, w_ssd_o, w_mla_o, w_out, w_ple, w_pg_down, w_pg_up):
    o = _IN_OFFS
    seg = lambda k: w_in[:, o[k]:o[k + 1]]
    w_a = jnp.concatenate([seg(0), seg(1), seg(3)], axis=1).astype(BF16)
    w_mid = jnp.concatenate([seg(4), seg(5), seg(6), seg(2)], axis=1).astype(BF16)
    w_g = jnp.concatenate([seg(7), seg(8)], axis=1).astype(BF16)
    wq = w_q_b.reshape(Q_LORA, MLA_HEADS, QK_NOPE + QK_ROPE)
    wq = jnp.pad(wq, ((0, 0), (0, 0), (0, QK_PAD - QK_NOPE - QK_ROPE)))
    wq = wq.reshape(Q_LORA, MLA_HEADS * QK_PAD).astype(BF16)
    wkv = w_kv_b.reshape(KV_LORA, MLA_HEADS, QK_NOPE + V_DIM)
    w_uk = wkv[..., :QK_NOPE].reshape(KV_LORA, MLA_HEADS * QK_NOPE).astype(BF16)
    w_uv = wkv[..., QK_NOPE:].reshape(KV_LORA, MLA_HEADS * V_DIM).astype(BF16)
    c = lambda w: w.astype(BF16)
    return dict(w_a=w_a, w_mid=w_mid, w_g=w_g, wq=wq, w_uk=w_uk, w_uv=w_uv, w_ssd_o=c(w_ssd_o),
                w_mla_o=c(w_mla_o), w_out=c(w_out), w_ple=c(w_ple), w_pg_down=c(w_pg_down),
                w_pg_up=c(w_pg_up))


def _in_proj(x, g_norm, wts, g_q, g_kv, cos_t, sin_t):
    h = _rmsnorm(x, g_norm)
    u_a = _mm(h, wts["w_a"])
    u_g = _mm(h, wts["w_g"])
    u_mid = _mm(h, wts["w_mid"], tm=min(x.shape[0], 512))
    cqn, ckv, ckv_b, kr, krp, dt_raw = _mid_post(u_mid, g_q, g_kv, cos_t, sin_t)
    q = _mm(cqn, wts["wq"], mode="rope_q", extras=(cos_t, sin_t), out_dtype=BF16,
            table_rows=cos_t.shape[0], tm=min(x.shape[0], cos_t.shape[0], 1024))
    return u_a, u_g, ckv, ckv_b, kr, krp, dt_raw, q


def _out_proj(x, y_ssd, o_gated, u_g, p_l, wts, g_ple):
    tn = 512
    ys = _mm(y_ssd, wts["w_ssd_o"], mode="sig_mul", extras=(u_g,), tn=tn)
    merged = _mm(o_gated, wts["w_mla_o"], mode="sig_mul_add", extras=(u_g, ys),
                 extra_col0=(D_MODEL // tn, 0), out_dtype=BF16, tn=tn)
    x = _mm(merged, wts["w_out"], mode="residual", extras=(x,), tn=tn)
    ge = _mm(_rmsnorm(x, g_ple), wts["w_pg_down"], mode="gelu", out_dtype=BF16)
    return _ple(ge, p_l, wts["w_pg_up"], wts["w_ple"], x)


def _layer_prompt(x, p_l, tables, lw, wts, batch, seq):
    (g_norm, conv_w, conv_b, dt_bias, a_log, d_skip, g_ssd, g_q, g_kv, g_ple) = lw
    u_a, u_g, ckv, ckv_b, kr, krp, dt_raw, q = _in_proj(x, g_norm, wts, g_q, g_kv, *tables)
    xbc = _conv_prompt(u_a, conv_w, conv_b, seq)
    y_ssd, ssm_new = _ssd_prompt(xbc, u_a, dt_raw, dt_bias, a_log, d_skip, g_ssd, batch, seq)
    kcat = _mm(ckv_b, wts["w_uk"], mode="kcat", extras=(krp,), out_dtype=BF16)
    v = _mm(ckv_b, wts["w_uv"], out_dtype=BF16)
    o_gated = _attn_prompt(q, kcat, v, u_a, batch, seq)
    x = _out_proj(x, y_ssd, o_gated, u_g, p_l, wts, g_ple)
    conv_new = u_a.reshape(batch, seq, -1)[:, seq - (CONV_W - 1):, SSD_INNER:SSD_INNER + CONV_DIM]
    return x, ckv, kr, ssm_new, conv_new


def _layer_sample(x, p_l, tables, lw, wts, conv_state, ssm_states, ssm_stack, cache_ckv, cache_krope_t,
                  page_table, layer):
    (g_norm, conv_w, conv_b, dt_bias, a_log, d_skip, g_ssd, g_q, g_kv, g_ple) = lw
    u_a, u_g, ckv, ckv_b, kr, krp, dt_raw, q = _in_proj(x, g_norm, wts, g_q, g_kv, *tables)
    y_ssd, ssm_stack, conv_new = _ssd_step(
        u_a[:, SSD_INNER:SSD_INNER + CONV_DIM], conv_state, ssm_states, ssm_stack, layer, u_a[:, :SSD_INNER],
        dt_raw, conv_w, conv_b, dt_bias, a_log, d_skip, g_ssd)
    qlat = _qlat(q, wts["w_uk"])
    o_lat = _decode_attn(qlat, ckv_b, krp, cache_ckv, cache_krope_t, page_table, layer)
    o_gated = _olat_proj(o_lat.reshape(x.shape[0], -1), wts["w_uv"], u_a[:, ZM_COL0:])
    x = _out_proj(x, y_ssd, o_gated, u_g, p_l, wts, g_ple)
    return x, ckv, kr, ssm_stack, conv_new


def kernel(x_prompt, x_sample, p_prompt, p_sample, cache_ckv, cache_krope, state_ssm, state_conv,
           page_table, g_norm, w_in, conv_w, conv_b, dt_bias, a_log, d_skip, g_ssd, w_ssd_o,
           g_q, w_q_b, g_kv, w_kv_b, w_mla_o, w_out, g_ple, w_ple, w_pg_down, w_pg_up, g_final):
    b_p, s_p, _ = x_prompt.shape
    b_s, s_s, _ = x_sample.shape
    assert s_s == 1
    depth = w_in.shape[0]
    past_len = page_table.shape[1] * cache_ckv.shape[2]
    tab_p = _rope_tables(jnp.arange(s_p))
    tab_s = _rope_tables(jnp.full((b_s,), past_len))
    hp = x_prompt.reshape(b_p * s_p, D_MODEL)
    hs = x_sample.reshape(b_s, D_MODEL)
    ssm_states = state_ssm.reshape(depth, b_s, SSD_INNER, SSD_STATE)
    ssm_stack = None
    cache_krope_t = cache_krope.swapaxes(2, 3)
    outs_p, outs_s = [], []
    for i in range(depth):
        wts = _prep_weights(w_in[i], w_q_b[i], w_kv_b[i], w_ssd_o[i], w_mla_o[i], w_out[i], w_ple[i],
                            w_pg_down[i], w_pg_up[i])
        lw = (g_norm[i], conv_w[i], conv_b[i], dt_bias[i], a_log[i], d_skip[i], g_ssd[i], g_q[i],
              g_kv[i], g_ple[i])
        hp, *o_p = _layer_prompt(hp, p_prompt[i].reshape(b_p * s_p, PLE_DIM), tab_p, lw, wts, b_p, s_p)
        hs, *o_s = _layer_sample(hs, p_sample[i].reshape(b_s, PLE_DIM), tab_s, lw, wts, state_conv[i],
                                 ssm_states, ssm_stack, cache_ckv, cache_krope_t, page_table, i)
        ssm_stack = o_s[2]
        outs_p.append(o_p)
        outs_s.append(o_s)
    y_prompt = _rmsnorm(hp, g_final, F32).reshape(b_p, s_p, D_MODEL)
    y_sample = _rmsnorm(hs, g_final, F32).reshape(b_s, s_s, D_MODEL)
    stack = lambda outs, k, shape: jnp.stack([o[k] for o in outs]).reshape((depth,) + shape)
    return (y_prompt, y_sample,
            stack(outs_p, 0, (b_p, s_p, KV_LORA)), stack(outs_p, 1, (b_p, s_p, QK_ROPE)),
            stack(outs_p, 2, (b_p, SSD_HEADS, SSD_HEADDIM, SSD_STATE)),
            stack(outs_p, 3, (b_p, CONV_W - 1, CONV_DIM)),
            stack(outs_s, 0, (b_s, s_s, KV_LORA)), stack(outs_s, 1, (b_s, s_s, QK_ROPE)),
            ssm_stack.reshape(depth, b_s, SSD_HEADS, SSD_HEADDIM, SSD_STATE),
            stack(outs_s, 3, (b_s, CONV_W - 1, CONV_DIM)))
```

```python
import functools
import math

import jax
import jax.numpy as jnp
import numpy as np
from jax import lax
from jax.experimental import pallas as pl
from jax.experimental.pallas import tpu as pltpu

F32 = jnp.float32
BF16 = jnp.bfloat16
HIGHEST = lax.Precision.HIGHEST

D_MODEL = 4096
SSD_INNER = 4096
SSD_HEADDIM = 64
SSD_HEADS = 64
SSD_GROUPS = 8
SSD_STATE = 128
SSD_HPG = SSD_HEADS // SSD_GROUPS
SSD_GW = SSD_HPG * SSD_HEADDIM
CONV_W = 4
CONV_DIM = SSD_INNER + 2 * SSD_GROUPS * SSD_STATE
SSD_CHUNK = 128
MLA_HEADS = 32
QK_NOPE = 128
QK_ROPE = 64
V_DIM = 128
Q_LORA = 768
KV_LORA = 512
MLA_WIDTH = MLA_HEADS * V_DIM
ROPE_THETA = 10000.0
PLE_DIM = 256
EPS = 1e-6
LANES = 128
SUBLANES = 8
QK_PAD = 2 * LANES
MID_W = Q_LORA + KV_LORA + LANES
QLAT_W = KV_LORA + LANES
PAGES_PER_STEP = 32
DECODE_GROUP = 32
ATTN_SCALE = 1.0 / math.sqrt(QK_NOPE + QK_ROPE)
VMEM_LIMIT = 56 * 1024 * 1024

_IN_SIZES = (SSD_INNER, CONV_DIM, SSD_HEADS, MLA_WIDTH, Q_LORA, KV_LORA, QK_ROPE, D_MODEL, D_MODEL)
_IN_OFFS = tuple(int(v) for v in np.cumsum((0,) + _IN_SIZES))
ZM_COL0 = SSD_INNER + CONV_DIM


def _cparams(*sem):
    return pltpu.CompilerParams(dimension_semantics=sem, vmem_limit_bytes=VMEM_LIMIT)


def _sigmoid(x):
    return 1.0 / (1.0 + jnp.exp(-x))


def _silu(x):
    return x * _sigmoid(x)


def _softplus(x):
    return jnp.maximum(x, 0.0) + jnp.log(1.0 + jnp.exp(-jnp.abs(x)))


def _rope128(x, cos_t, sin_t):
    lane = lax.broadcasted_iota(jnp.int32, x.shape, 1)
    rot = jnp.where(lane < QK_ROPE // 2, -pltpu.roll(x, LANES - QK_ROPE // 2, 1),
                    pltpu.roll(x, QK_ROPE // 2, 1))
    return x * cos_t + rot * sin_t


def _dot_nt(a, b):
    return lax.dot_general(a, b, (((1,), (1,)), ((), ())), preferred_element_type=F32)


def _rmsnorm_kernel(x_ref, g_ref, o_ref):
    x = x_ref[...]
    y = x * lax.rsqrt(jnp.mean(x * x, axis=-1, keepdims=True) + EPS)
    o_ref[...] = (y * g_ref[...]).astype(o_ref.dtype)


def _rmsnorm(x, g, out_dtype=BF16):
    m, k = x.shape
    tm = min(m, 256)
    return pl.pallas_call(
        _rmsnorm_kernel,
        out_shape=jax.ShapeDtypeStruct((m, k), out_dtype),
        grid=(m // tm,),
        in_specs=[pl.BlockSpec((tm, k), lambda i: (i, 0)),
                  pl.BlockSpec((1, k), lambda i: (0, 0))],
        out_specs=pl.BlockSpec((tm, k), lambda i: (i, 0)),
        compiler_params=_cparams("parallel"),
        name="rmsnorm",
    )(x, g.reshape(1, k))


def _mm_kernel(a_ref, w_ref, *refs, mode):
    o_ref = refs[-1]
    acc = jnp.dot(a_ref[...], w_ref[...], preferred_element_type=F32)
    if mode == "plain":
        out = acc
    elif mode == "gelu":
        out = jax.nn.gelu(acc)
    elif mode == "sig_mul":
        out = _sigmoid(refs[0][...]) * acc
    elif mode == "sig_mul_add":
        out = _sigmoid(refs[0][...]) * acc + refs[1][...]
    elif mode == "silu_mul":
        out = _silu(refs[0][...]) * acc
    elif mode == "residual":
        out = refs[0][...] + acc
    elif mode == "rope_q":
        cos_t, sin_t = refs[0][...], refs[1][...]
        acc = acc * ATTN_SCALE
        for c in range(acc.shape[1] // QK_PAD):
            lo = c * QK_PAD
            o_ref[:, lo:lo + LANES] = acc[:, lo:lo + LANES].astype(o_ref.dtype)
            o_ref[:, lo + LANES:lo + QK_PAD] = _rope128(
                acc[:, lo + LANES:lo + QK_PAD], cos_t, sin_t).astype(o_ref.dtype)
        return
    elif mode == "kcat":
        krp = refs[0][...]
        for c in range(acc.shape[1] // LANES):
            o_ref[:, c * QK_PAD:c * QK_PAD + LANES] = acc[:, c * LANES:(c + 1) * LANES].astype(o_ref.dtype)
            o_ref[:, c * QK_PAD + LANES:(c + 1) * QK_PAD] = krp
        return
    else:
        raise ValueError(mode)
    o_ref[...] = out.astype(o_ref.dtype)


def _mm(a, w, layer, *, mode="plain", extras=(), extra_col0=(), out_dtype=F32, tm=None, tn=512, n_out=None,
        table_rows=None):
    m = a.shape[0]
    _, k, n = w.shape
    n = n if n_out is None else n_out
    if tm is None:
        tm = min(m, 1024)
    tn = min(tn, n)
    assert m % tm == 0 and a.shape[1] == k
    grid = (m // tm, pl.cdiv(n, tn))
    in_specs = [pl.BlockSpec((tm, k), lambda i, j: (i, 0)),
                pl.BlockSpec((None, k, tn), lambda i, j: (layer, 0, j))]
    out_n, out_tn = n, tn
    if mode == "rope_q":
        assert tn % QK_PAD == 0 and table_rows % tm == 0
        nt = table_rows // tm
        in_specs += [pl.BlockSpec((tm, LANES), lambda i, j: (i % nt, 0))] * 2
    elif mode == "kcat":
        in_specs += [pl.BlockSpec((tm, LANES), lambda i, j: (i, 0))]
        out_n, out_tn = 2 * n, 2 * tn
    else:
        for c0 in (tuple(extra_col0) + (0,) * len(extras))[:len(extras)]:
            in_specs.append(pl.BlockSpec((tm, tn), functools.partial(lambda i, j, c: (i, j + c), c=c0)))
    return pl.pallas_call(
        functools.partial(_mm_kernel, mode=mode),
        out_shape=jax.ShapeDtypeStruct((m, out_n), out_dtype),
        grid=grid,
        in_specs=in_specs,
        out_specs=pl.BlockSpec((tm, out_tn), lambda i, j: (i, j)),
        compiler_params=_cparams("parallel", "arbitrary"),
        name="mm_" + mode,
    )(a, w, *extras)


def _mid_kernel(u_ref, gq_ref, gkv_ref, cos_ref, sin_ref,
                cqn_ref, ckv_ref, ckvb_ref, kr_ref, krp_ref, dt_ref):
    def norm(x, g):
        return x * lax.rsqrt(jnp.mean(x * x, axis=-1, keepdims=True) + EPS) * g

    cqn_ref[...] = norm(u_ref[:, :Q_LORA], gq_ref[...]).astype(cqn_ref.dtype)
    ckv = norm(u_ref[:, Q_LORA:Q_LORA + KV_LORA], gkv_ref[...])
    ckv_ref[...] = ckv
    ckvb_ref[...] = ckv.astype(ckvb_ref.dtype)
    krdt = u_ref[:, Q_LORA + KV_LORA:]
    lane = lax.broadcasted_iota(jnp.int32, krdt.shape, 1)
    kr = _rope128(jnp.where(lane < QK_ROPE, krdt, 0.0), cos_ref[...], sin_ref[...])
    kr_ref[...] = kr[:, :QK_ROPE]
    krp_ref[...] = kr.astype(krp_ref.dtype)
    dt_ref[...] = pltpu.roll(krdt, QK_ROPE, 1)[:, :SSD_HEADS]


def _mid_post(u_mid, g_q, g_kv, cos_t, sin_t):
    m = u_mid.shape[0]
    table_rows = cos_t.shape[0]
    tm = min(m, table_rows, 256)
    nt = table_rows // tm
    row = lambda i: (i, 0)
    return pl.pallas_call(
        _mid_kernel,
        out_shape=(jax.ShapeDtypeStruct((m, Q_LORA), BF16),
                   jax.ShapeDtypeStruct((m, KV_LORA), F32),
                   jax.ShapeDtypeStruct((m, KV_LORA), BF16),
                   jax.ShapeDtypeStruct((m, QK_ROPE), F32),
                   jax.ShapeDtypeStruct((m, LANES), BF16),
                   jax.ShapeDtypeStruct((m, SSD_HEADS), F32)),
        grid=(m // tm,),
        in_specs=[pl.BlockSpec((tm, MID_W), row),
                  pl.BlockSpec((1, Q_LORA), lambda i: (0, 0)),
                  pl.BlockSpec((1, KV_LORA), lambda i: (0, 0)),
                  pl.BlockSpec((tm, LANES), lambda i: (i % nt, 0)),
                  pl.BlockSpec((tm, LANES), lambda i: (i % nt, 0))],
        out_specs=(pl.BlockSpec((tm, Q_LORA), row), pl.BlockSpec((tm, KV_LORA), row),
                   pl.BlockSpec((tm, KV_LORA), row), pl.BlockSpec((tm, QK_ROPE), row),
                   pl.BlockSpec((tm, LANES), row), pl.BlockSpec((tm, SSD_HEADS), row)),
        compiler_params=_cparams("parallel"),
        name="mid_post",
    )(u_mid, g_q.reshape(1, -1), g_kv.reshape(1, -1), cos_t, sin_t)


def _conv_kernel(x_ref, halo_ref, w_ref, b_ref, o_ref, *, tiles_per_seq):
    i = pl.program_id(1)
    x = x_ref[...]
    halo = jnp.where(i % tiles_per_seq == 0, 0.0, halo_ref[...])
    ext = jnp.concatenate([halo, x], axis=0)
    w = w_ref[...]
    y = b_ref[...] + x * w[CONV_W - 1:CONV_W, :]
    for s in range(1, CONV_W):
        y = y + pltpu.roll(ext, s, 0)[SUBLANES:, :] * w[CONV_W - 1 - s:CONV_W - s, :]
    o_ref[...] = _silu(y).astype(o_ref.dtype)


def _conv_prompt(u_a, conv_w, conv_b, seq):
    m = u_a.shape[0]
    tt = min(seq, 512)
    tc = 512
    c0 = SSD_INNER // tc
    hb = tt // SUBLANES
    return pl.pallas_call(
        functools.partial(_conv_kernel, tiles_per_seq=seq // tt),
        out_shape=jax.ShapeDtypeStruct((m, CONV_DIM), BF16),
        grid=(CONV_DIM // tc, m // tt),
        in_specs=[pl.BlockSpec((tt, tc), lambda j, i: (i, c0 + j)),
                  pl.BlockSpec((SUBLANES, tc), lambda j, i: (jnp.maximum(i * hb - 1, 0), c0 + j)),
                  pl.BlockSpec((CONV_W, tc), lambda j, i: (0, j)),
                  pl.BlockSpec((1, tc), lambda j, i: (0, j))],
        out_specs=pl.BlockSpec((tt, tc), lambda j, i: (i, j)),
        compiler_params=_cparams("parallel", "arbitrary"),
        name="conv_prompt",
    )(u_a, u_a, conv_w, conv_b.reshape(1, -1))


def _ssd_kernel(x_ref, b_ref, c_ref, z_ref, dt_ref, dtt_ref, bias_ref, alog_ref, biast_ref, alogt_ref,
                dskip_ref, g_ref, e_ref, y_ref, st_out_ref, st_ref, yd_ref):
    c = pl.program_id(1)
    q = SSD_CHUNK

    @pl.when(c == 0)
    def _():
        st_ref[...] = jnp.zeros_like(st_ref)

    dt = _softplus(dt_ref[...] + bias_ref[...])
    dta = dt * -jnp.exp(alog_ref[...])
    dtt = _softplus(dtt_ref[...] + biast_ref[...])
    dtat = dtt * -jnp.exp(alogt_ref[...])
    row = lax.broadcasted_iota(jnp.int32, (q, q), 0)
    col = lax.broadcasted_iota(jnp.int32, (q, q), 1)
    tri = row >= col
    cum = jnp.dot(tri.astype(F32), dta, precision=HIGHEST, preferred_element_type=F32)
    cumt = jnp.dot(dtat, (row <= col).astype(F32), precision=HIGHEST, preferred_element_type=F32)
    cum_last = cum[q - 1:q, :]
    fac = jnp.concatenate([jnp.exp(cum), jnp.exp(cum_last - cum) * dt,
                           jnp.broadcast_to(jnp.exp(cum_last), (SUBLANES, SSD_HEADS))], axis=0)
    fac_hi = fac.astype(BF16)
    fac_lo = (fac - fac_hi.astype(F32)).astype(BF16)
    e = e_ref[...]
    fac = (jnp.dot(fac_hi, e, preferred_element_type=F32) + jnp.dot(fac_lo, e, preferred_element_type=F32))
    e_cum = fac[:q]
    dw = fac[q:2 * q]
    dec = fac[2 * q:2 * q + 1]

    for g in range(SSD_GROUPS):
        gl = slice(g * SSD_GW, (g + 1) * SSD_GW)
        nl = slice(g * SSD_STATE, (g + 1) * SSD_STATE)
        bg = b_ref[:, nl]
        cg = c_ref[:, nl]
        cb = _dot_nt(cg, bg)
        xg_b = x_ref[:, gl]
        xg = xg_b.astype(F32)
        st_g = st_ref[:, gl]
        y_off = jnp.dot(cg, st_g.astype(BF16), preferred_element_type=F32) * e_cum[:, gl]
        for r in range(SSD_HPG):
            h = g * SSD_HPG + r
            lmat = jnp.exp(jnp.where(tri, cum[:, h:h + 1] - cumt[h:h + 1, :], -jnp.inf))
            mm = (cb * lmat * dtt[h:h + 1, :]).astype(BF16)
            yd_ref[:, r * SSD_HEADDIM:(r + 1) * SSD_HEADDIM] = jnp.dot(
                mm, xg_b[:, r * SSD_HEADDIM:(r + 1) * SSD_HEADDIM], preferred_element_type=F32)
        xw = (xg * dw[:, gl]).astype(BF16)
        st_ref[:, gl] = st_g * dec[:, gl] + jnp.dot(bg.astype(F32).T.astype(BF16), xw,
                                                         preferred_element_type=F32)
        y = yd_ref[...] + y_off + dskip_ref[:, gl] * xg
        v = y * _silu(z_ref[:, gl])
        v = v * lax.rsqrt(jnp.mean(v * v, axis=-1, keepdims=True) + EPS)
        y_ref[:, gl] = (v * g_ref[:, gl]).astype(y_ref.dtype)

    @pl.when(c == pl.num_programs(1) - 1)
    def _():
        st_out_ref[0] = st_ref[...].T


def _head_expand():
    return jnp.asarray(np.repeat(np.eye(SSD_HEADS, dtype=np.float32), SSD_HEADDIM, axis=1), dtype=BF16)


def _ssd_prompt(xbc, u_a, dt_raw, dt_bias, a_log, d_skip, g_ssd, batch, seq):
    m = xbc.shape[0]
    q = SSD_CHUNK
    nc = seq // q
    rowblk = lambda b, c: (b * nc + c, 0)
    const = lambda b, c: (0, 0)
    bc0 = SSD_INNER // (SSD_GROUPS * SSD_STATE)
    y, st = pl.pallas_call(
        _ssd_kernel,
        out_shape=(jax.ShapeDtypeStruct((m, SSD_INNER), BF16),
                   jax.ShapeDtypeStruct((batch, SSD_INNER, SSD_STATE), F32)),
        grid=(batch, nc),
        in_specs=[pl.BlockSpec((q, SSD_INNER), rowblk),
                  pl.BlockSpec((q, SSD_GROUPS * SSD_STATE), lambda b, c: (b * nc + c, bc0)),
                  pl.BlockSpec((q, SSD_GROUPS * SSD_STATE), lambda b, c: (b * nc + c, bc0 + 1)),
                  pl.BlockSpec((q, SSD_INNER), rowblk),
                  pl.BlockSpec((q, SSD_HEADS), rowblk),
                  pl.BlockSpec((SSD_HEADS, q), lambda b, c: (0, b * nc + c)),
                  pl.BlockSpec((1, SSD_HEADS), const), pl.BlockSpec((1, SSD_HEADS), const),
                  pl.BlockSpec((SSD_HEADS, 1), const), pl.BlockSpec((SSD_HEADS, 1), const),
                  pl.BlockSpec((1, SSD_INNER), const), pl.BlockSpec((1, SSD_INNER), const),
                  pl.BlockSpec((SSD_HEADS, SSD_INNER), const)],
        out_specs=(pl.BlockSpec((q, SSD_INNER), rowblk),
                   pl.BlockSpec((1, SSD_INNER, SSD_STATE), lambda b, c: (b, 0, 0))),
        scratch_shapes=[pltpu.VMEM((SSD_STATE, SSD_INNER), F32), pltpu.VMEM((q, SSD_GW), F32)],
        compiler_params=_cparams("parallel", "arbitrary"),
        name="ssd_prompt",
    )(xbc, xbc, xbc, u_a, dt_raw, dt_raw.T, dt_bias.reshape(1, -1), a_log.reshape(1, -1),
      dt_bias.reshape(-1, 1), a_log.reshape(-1, 1), jnp.repeat(d_skip, SSD_HEADDIM).reshape(1, -1),
      g_ssd.reshape(1, -1), _head_expand())
    return y, st.reshape(batch, SSD_HEADS, SSD_HEADDIM, SSD_STATE)


def _attn_kernel(q_ref, k_ref, v_ref, z_ref, o_ref, s0_ref, s1_ref, *, tq):
    seq = q_ref.shape[0]
    row = lax.broadcasted_iota(jnp.int32, (tq, tq), 0)
    col = lax.broadcasted_iota(jnp.int32, (tq, tq), 1)
    for qi in range(seq // tq):
        s_ref = (s0_ref, s1_ref)[qi % 2]
        rows = slice(qi * tq, (qi + 1) * tq)
        q = q_ref[rows, :]
        m = None
        for j in range(qi + 1):
            cols = slice(j * tq, (j + 1) * tq)
            s = _dot_nt(q, k_ref[cols, :])
            if j == qi:
                s = jnp.where(col <= row, s, -jnp.inf)
            s_ref[:, cols] = s
            m_j = jnp.max(s, axis=-1, keepdims=True)
            m = m_j if m is None else jnp.maximum(m, m_j)
        l = acc = None
        for j in range(qi + 1):
            cols = slice(j * tq, (j + 1) * tq)
            p = jnp.exp(s_ref[:, cols] - m)
            l_j = jnp.sum(p, axis=-1, keepdims=True)
            pv = jnp.dot(p.astype(BF16), v_ref[cols, :], preferred_element_type=F32)
            l = l_j if l is None else l + l_j
            acc = pv if acc is None else acc + pv
        o_ref[rows, :] = (acc / l * _silu(z_ref[rows, :])).astype(o_ref.dtype)


def _attn_prompt(q, kcat, v, z_m, batch, seq):
    m = q.shape[0]
    tq = min(seq, 512)
    return pl.pallas_call(
        functools.partial(_attn_kernel, tq=tq),
        out_shape=jax.ShapeDtypeStruct((m, MLA_WIDTH), BF16),
        grid=(batch, MLA_HEADS),
        in_specs=[pl.BlockSpec((seq, QK_PAD), lambda b, h: (b, h)),
                  pl.BlockSpec((seq, QK_PAD), lambda b, h: (b, h)),
                  pl.BlockSpec((seq, V_DIM), lambda b, h: (b, h)),
                  pl.BlockSpec((seq, V_DIM), lambda b, h: (b, h))],
        out_specs=pl.BlockSpec((seq, V_DIM), lambda b, h: (b, h)),
        scratch_shapes=[pltpu.VMEM((tq, seq), F32), pltpu.VMEM((tq, seq), F32)],
        compiler_params=_cparams("parallel", "arbitrary"),
        name="attn_prompt",
    )(q, kcat, v, z_m)


def _ple_kernel(ge_ref, p_ref, wu_ref, wp_ref, x_ref, o_ref):
    gate = _sigmoid(jnp.dot(ge_ref[...], wu_ref[...], preferred_element_type=F32))
    emb = jnp.dot(p_ref[...].astype(BF16), wp_ref[...], preferred_element_type=F32)
    o_ref[...] = x_ref[...] + gate * emb


def _ple(ge, p, w_up, w_ple, layer, x):
    m, n = x.shape
    tm = min(m, 512)
    tn = 1024
    r = ge.shape[1]
    return pl.pallas_call(
        _ple_kernel,
        out_shape=jax.ShapeDtypeStruct((m, n), F32),
        grid=(m // tm, n // tn),
        in_specs=[pl.BlockSpec((tm, r), lambda i, j: (i, 0)),
                  pl.BlockSpec((tm, PLE_DIM), lambda i, j: (i, 0)),
                  pl.BlockSpec((None, r, tn), lambda i, j: (layer, 0, j)),
                  pl.BlockSpec((None, PLE_DIM, tn), lambda i, j: (layer, 0, j)),
                  pl.BlockSpec((tm, tn), lambda i, j: (i, j))],
        out_specs=pl.BlockSpec((tm, tn), lambda i, j: (i, j)),
        compiler_params=_cparams("parallel", "arbitrary"),
        name="ple",
    )(ge, p, w_up, w_ple, x)


def _ssd_step_kernel(xbc_ref, cst_ref, xt_ref, cstt_ref, st_ref, zt_ref, dt_ref, dtx_ref, w_ref, cb_ref,
                     wt_ref, cbt_ref, bias_ref, biasx_ref, alog_ref, dskip_ref, gt_ref, grp_ref, *refs):
    conv_out_ref, st_out_ref, y_ref = refs[-3:]
    hp = SSD_HEADDIM
    w = w_ref[...]
    x_new = xbc_ref[0]
    old = cst_ref[0]
    y = cb_ref[...] + x_new * w[CONV_W - 1:CONV_W, :]
    for k in range(CONV_W - 1):
        y = y + old[k:k + 1, :] * w[k:k + 1, :]
    for k in range(CONV_W - 2):
        conv_out_ref[0, k:k + 1, :] = old[k + 1:k + 2, :]
    conv_out_ref[0, CONV_W - 2:CONV_W - 1, :] = x_new
    xbc = _silu(y)
    xdt = xbc[:, :SSD_INNER] * _softplus(dtx_ref[0] + biasx_ref[...])
    wt = wt_ref[...]
    xt = cbt_ref[...] + xt_ref[0] * wt[CONV_W - 1]
    for k in range(CONV_W - 1):
        xt = xt + cstt_ref[0, k] * wt[k]
    xt = _silu(xt)

    dt = _softplus(dt_ref[0] + bias_ref[...])
    da = jnp.exp(dt * -jnp.exp(alog_ref[...]))
    eye_h = (lax.broadcasted_iota(jnp.int32, (SSD_HEADS, SSD_HEADS), 0)
             == lax.broadcasted_iota(jnp.int32, (SSD_HEADS, SSD_HEADS), 1)).astype(F32)
    dab = lax.dot_general(eye_h, jnp.broadcast_to(da, (SSD_STATE, SSD_HEADS)), (((1,), (1,)), ((), ())),
                          precision=HIGHEST, preferred_element_type=F32)
    rowi = lax.broadcasted_iota(jnp.int32, (hp, LANES), 0)
    lanei = lax.broadcasted_iota(jnp.int32, (hp, LANES), 1)
    diag = (rowi == lanei, rowi + hp == lanei)
    lane_h = lax.broadcasted_iota(jnp.int32, (hp, SSD_HEADS), 1)
    yt = jnp.zeros((hp, SSD_HEADS), F32)
    for g in range(SSD_GROUPS):
        b0 = SSD_INNER + g * SSD_STATE
        c0 = SSD_INNER + (SSD_GROUPS + g) * SSD_STATE
        b_mat = jnp.broadcast_to(xbc[:, b0:b0 + SSD_STATE], (LANES, SSD_STATE)).astype(BF16)
        c_mat = jnp.broadcast_to(xbc[:, c0:c0 + SSD_STATE], (LANES, SSD_STATE)).astype(BF16)
        a_parts, da_parts = [], []
        for r in range(SSD_HPG):
            h = g * SSD_HPG + r
            pair = xdt[:, (h // 2) * LANES:(h // 2 + 1) * LANES]
            a_parts.append(jnp.where(diag[h % 2], pair, 0.0).astype(BF16))
            da_parts.append(jnp.broadcast_to(dab[h:h + 1, :], (hp, SSD_STATE)))
        rows = slice(g * SSD_GW, (g + 1) * SSD_GW)
        inc = jnp.dot(jnp.concatenate(a_parts, axis=0), b_mat, preferred_element_type=F32)
        new = st_ref[0, rows, :] * jnp.concatenate(da_parts, axis=0) + inc
        st_out_ref[0, rows, :] = new
        yb = _dot_nt(new.astype(BF16), c_mat)
        for r in range(SSD_HPG):
            h = g * SSD_HPG + r
            yt = jnp.where(lane_h == h, yb[r * hp:(r + 1) * hp, :SSD_HEADS], yt)
    yt = yt + dskip_ref[...] * xt
    v = yt * _silu(zt_ref[0])
    ss = jnp.sum(v * v, axis=0, keepdims=True)
    ss = jnp.dot(jnp.broadcast_to(ss, (SUBLANES, SSD_HEADS)), grp_ref[...], precision=HIGHEST,
                 preferred_element_type=F32)[0:1, :]
    v = v * lax.rsqrt(ss / (SSD_INNER // SSD_GROUPS) + EPS)
    y_ref[0] = (v * gt_ref[...]).astype(y_ref.dtype)


def _ssd_step(xbc_new, conv_state, ssm_states, ssm_stack, layer, z_s, dt_raw, conv_w, conv_b, dt_bias, a_log,
              d_skip, g_ssd):
    nb = xbc_new.shape[0]
    hp = SSD_HEADDIM
    to_t = lambda a: a.reshape(a.shape[:-1] + (SSD_HEADS, hp)).swapaxes(-1, -2)
    rep = lambda a: jnp.repeat(a, hp, axis=-1)
    grp = jnp.asarray(np.kron(np.eye(SSD_GROUPS, dtype=np.float32), np.ones((SSD_HPG, SSD_HPG), np.float32)))
    const2 = lambda b: (0, 0)
    const3 = lambda b: (0, 0, 0)
    blk3 = lambda b: (b, 0, 0)
    blk4 = lambda b: (b, 0, 0, 0)
    st_spec = pl.BlockSpec((None, 1, SSD_INNER, SSD_STATE), lambda b: (layer, b, 0, 0))
    in_specs = [pl.BlockSpec((1, 1, CONV_DIM), blk3),
                pl.BlockSpec((1, CONV_W - 1, CONV_DIM), blk3),
                pl.BlockSpec((1, hp, SSD_HEADS), blk3),
                pl.BlockSpec((1, CONV_W - 1, hp, SSD_HEADS), blk4),
                st_spec,
                pl.BlockSpec((1, hp, SSD_HEADS), blk3),
                pl.BlockSpec((1, 1, SSD_HEADS), blk3),
                pl.BlockSpec((1, 1, SSD_INNER), blk3),
                pl.BlockSpec((CONV_W, CONV_DIM), const2), pl.BlockSpec((1, CONV_DIM), const2),
                pl.BlockSpec((CONV_W, hp, SSD_HEADS), const3), pl.BlockSpec((hp, SSD_HEADS), const2),
                pl.BlockSpec((1, SSD_HEADS), const2), pl.BlockSpec((1, SSD_INNER), const2),
                pl.BlockSpec((1, SSD_HEADS), const2), pl.BlockSpec((1, SSD_HEADS), const2),
                pl.BlockSpec((hp, SSD_HEADS), const2), pl.BlockSpec((SSD_HEADS, SSD_HEADS), const2)]
    args = [xbc_new.reshape(nb, 1, CONV_DIM), conv_state, to_t(xbc_new[:, :SSD_INNER]),
            to_t(conv_state[:, :, :SSD_INNER]), ssm_states, to_t(z_s), dt_raw.reshape(nb, 1, SSD_HEADS),
            rep(dt_raw).reshape(nb, 1, SSD_INNER), conv_w, conv_b.reshape(1, -1), to_t(conv_w[:, :SSD_INNER]),
            to_t(conv_b[:SSD_INNER]), dt_bias.reshape(1, -1), rep(dt_bias).reshape(1, -1), a_log.reshape(1, -1),
            d_skip.reshape(1, -1), to_t(g_ssd), grp]
    in_specs.append(pl.BlockSpec(memory_space=pl.ANY))
    args.append(ssm_stack)
    aliases = {len(args) - 1: 1}
    conv_new, ssm_stack, yt = pl.pallas_call(
        _ssd_step_kernel,
        out_shape=(jax.ShapeDtypeStruct((nb, CONV_W - 1, CONV_DIM), F32),
                   jax.ShapeDtypeStruct(ssm_states.shape, F32),
                   jax.ShapeDtypeStruct((nb, hp, SSD_HEADS), BF16)),
        grid=(nb,),
        in_specs=in_specs,
        out_specs=(pl.BlockSpec((1, CONV_W - 1, CONV_DIM), blk3), st_spec,
                   pl.BlockSpec((1, hp, SSD_HEADS), blk3)),
        input_output_aliases=aliases,
        compiler_params=_cparams("arbitrary"),
        name="ssd_step",
    )(*args)
    y = yt.swapaxes(1, 2).reshape(nb, SSD_INNER)
    return y, ssm_stack, conv_new


def _qlat_kernel(q_ref, w_ref, o_ref):
    q = q_ref[...]
    o_ref[:, :KV_LORA] = _dot_nt(q[:, :QK_NOPE], w_ref[...]).astype(o_ref.dtype)
    o_ref[:, KV_LORA:] = q[:, QK_NOPE:]


def _qlat(q, w_uk, layer):
    nb = q.shape[0]
    out = pl.pallas_call(
        _qlat_kernel,
        out_shape=jax.ShapeDtypeStruct((nb, MLA_HEADS * QLAT_W), BF16),
        grid=(MLA_HEADS,),
        in_specs=[pl.BlockSpec((nb, QK_PAD), lambda h: (0, h)),
                  pl.BlockSpec((None, KV_LORA, QK_NOPE), lambda h: (layer, 0, h))],
        out_specs=pl.BlockSpec((nb, QLAT_W), lambda h: (0, h)),
        compiler_params=_cparams("parallel"),
        name="qlat",
    )(q, w_uk)
    return out.reshape(nb, MLA_HEADS, QLAT_W)


def _decode_attn_kernel(pt_ref, q_ref, cn_ref, kn_ref, ckv_hbm, kr_hbm, o_ref,
                        ckv_buf, kr_buf, sem, m_ref, l_ref, acc_ref, *, npg, layer):
    b = pl.program_id(0)
    s_idx = pl.program_id(1)
    n_b = pl.num_programs(0)
    n_s = pl.num_programs(1)
    t = b * n_s + s_idx
    slot = t % 2

    def page_copies(page, sl, k):
        return (pltpu.make_async_copy(ckv_hbm.at[layer, page], ckv_buf.at[sl, k], sem.at[0, sl]),
                pltpu.make_async_copy(kr_hbm.at[layer, page], kr_buf.at[sl, k], sem.at[1, sl]))

    def start_pages(bb, ss, sl, k0, k1):
        for k in range(k0, k1):
            for cp in page_copies(pt_ref[bb, ss * npg + k], sl, k):
                cp.start()

    def wait_pages(sl):
        for k in range(npg):
            for cp in page_copies(0, sl, k):
                cp.wait()

    @pl.when(t == 0)
    def _():
        start_pages(0, 0, 0, 0, npg)

    @pl.when(s_idx == 0)
    def _():
        m_ref[...] = jnp.full_like(m_ref, -jnp.inf)
        l_ref[...] = jnp.zeros_like(l_ref)
        acc_ref[...] = jnp.zeros_like(acc_ref)

    wait_pages(slot)
    wrap = s_idx + 1 == n_s
    nb_idx = jnp.minimum(jnp.where(wrap, b + 1, b), n_b - 1)
    ns_idx = jnp.where(wrap, jnp.where(b + 1 == n_b, s_idx, 0), s_idx + 1)

    q = q_ref[0]
    q_lat = q[:, :KV_LORA]
    q_rope = q[:, KV_LORA:KV_LORA + QK_ROPE]
    parts = []
    for g0 in range(0, npg, DECODE_GROUP):
        grp = range(g0, min(g0 + DECODE_GROUP, npg))
        start_pages(nb_idx, ns_idx, 1 - slot, grp.start, grp.stop)
        pages = [ckv_buf[slot, k].astype(BF16) for k in grp]
        s = jnp.concatenate(
            [_dot_nt(q_lat, pg) + jnp.dot(q_rope, kr_buf[slot, k].astype(BF16), preferred_element_type=F32)
             for pg, k in zip(pages, grp)], axis=-1)
        m_g = jnp.max(s, axis=-1, keepdims=True)
        pb = jnp.exp(s - m_g)
        l_g = jnp.sum(pb, axis=-1, keepdims=True)
        pb = pb.astype(BF16)
        psz = pages[0].shape[0]
        pv = jnp.dot(pb[:, :psz], pages[0], preferred_element_type=F32)
        for k in range(1, len(pages)):
            pv = pv + jnp.dot(pb[:, k * psz:(k + 1) * psz], pages[k], preferred_element_type=F32)
        parts.append((m_g, l_g, pv))
    m_old = m_ref[...]
    m_new = m_old
    for m_g, _, _ in parts:
        m_new = jnp.maximum(m_new, m_g)
    alpha = jnp.exp(m_old - m_new)
    l = alpha * l_ref[...]
    acc = alpha * acc_ref[...]
    for m_g, l_g, pv in parts:
        w_g = jnp.exp(m_g - m_new)
        l = l + w_g * l_g
        acc = acc + w_g * pv
    l_ref[...] = l
    acc_ref[...] = acc
    m_ref[...] = m_new

    @pl.when(s_idx == pl.num_programs(1) - 1)
    def _():
        cn = cn_ref[0].astype(F32)
        kn = kn_ref[0].astype(F32)
        s_new = (jnp.sum(q_lat.astype(F32) * cn, axis=-1, keepdims=True)
                 + jnp.sum(q[:, KV_LORA:].astype(F32) * kn, axis=-1, keepdims=True))
        m_old = m_ref[...]
        m_fin = jnp.maximum(m_old, s_new)
        alpha = jnp.exp(m_old - m_fin)
        p_new = jnp.exp(s_new - m_fin)
        l = alpha * l_ref[...] + p_new
        acc = alpha * acc_ref[...] + p_new.astype(BF16).astype(F32) * cn
        o_ref[0] = (acc / l).astype(o_ref.dtype)

    @pl.when(t == n_b * n_s - 1)
    def _():
        wait_pages(1 - slot)


def _decode_attn(qlat, ckv_new_b, krp_new, cache_ckv, cache_krope_t, page_table, layer):
    nb, n_pages = page_table.shape
    psz = cache_ckv.shape[2]
    npg = min(PAGES_PER_STEP, n_pages)
    assert n_pages % npg == 0
    grid_spec = pltpu.PrefetchScalarGridSpec(
        num_scalar_prefetch=1,
        grid=(nb, n_pages // npg),
        in_specs=[pl.BlockSpec((1, MLA_HEADS, QLAT_W), lambda b, s, pt: (b, 0, 0)),
                  pl.BlockSpec((1, 1, KV_LORA), lambda b, s, pt: (b, 0, 0)),
                  pl.BlockSpec((1, 1, LANES), lambda b, s, pt: (b, 0, 0)),
                  pl.BlockSpec(memory_space=pl.ANY), pl.BlockSpec(memory_space=pl.ANY)],
        out_specs=pl.BlockSpec((1, MLA_HEADS, KV_LORA), lambda b, s, pt: (b, 0, 0)),
        scratch_shapes=[pltpu.VMEM((2, npg, psz, KV_LORA), F32), pltpu.VMEM((2, npg, QK_ROPE, psz), F32),
                        pltpu.SemaphoreType.DMA((2, 2)),
                        pltpu.VMEM((MLA_HEADS, 1), F32), pltpu.VMEM((MLA_HEADS, 1), F32),
                        pltpu.VMEM((MLA_HEADS, KV_LORA), F32)],
    )
    return pl.pallas_call(
        functools.partial(_decode_attn_kernel, npg=npg, layer=layer),
        out_shape=jax.ShapeDtypeStruct((nb, MLA_HEADS, KV_LORA), BF16),
        grid_spec=grid_spec,
        compiler_params=_cparams("arbitrary", "arbitrary"),
        name="decode_attn",
    )(page_table, qlat, ckv_new_b.reshape(nb, 1, KV_LORA), krp_new.reshape(nb, 1, LANES),
      cache_ckv, cache_krope_t)


def _olat_kernel(o_ref, w_ref, z_ref, out_ref):
    acc = jnp.dot(o_ref[...], w_ref[...], preferred_element_type=F32)
    out_ref[...] = (acc * _silu(z_ref[...])).astype(out_ref.dtype)


def _olat_proj(o_lat, w_uv, layer, z_m):
    nb = o_lat.shape[0]
    return pl.pallas_call(
        _olat_kernel,
        out_shape=jax.ShapeDtypeStruct((nb, MLA_WIDTH), BF16),
        grid=(MLA_HEADS,),
        in_specs=[pl.BlockSpec((nb, KV_LORA), lambda h: (0, h)),
                  pl.BlockSpec((None, KV_LORA, V_DIM), lambda h: (layer, 0, h)),
                  pl.BlockSpec((nb, V_DIM), lambda h: (0, h))],
        out_specs=pl.BlockSpec((nb, V_DIM), lambda h: (0, h)),
        compiler_params=_cparams("parallel"),
        name="olat_proj",
    )(o_lat, w_uv, z_m)


def _rope_tables(pos):
    half = QK_ROPE // 2
    inv = ROPE_THETA ** (-jnp.arange(half, dtype=F32) / half)
    ang = pos.astype(F32)[:, None] * inv[None, :]
    zeros = jnp.zeros((pos.shape[0], LANES - QK_ROPE), F32)
    cos_t = jnp.concatenate([jnp.cos(ang), jnp.cos(ang), zeros], axis=-1)
    sin_t = jnp.concatenate([jnp.sin(ang), jnp.sin(ang), zeros], axis=-1)
    return cos_t, sin_t


def _prep_weights(w_in, w_q_b, w_kv_b, w_ssd_o, w_mla_o, w_out, w_ple, w_pg_down, w_pg_up):
    o = _IN_OFFS
    depth = w_in.shape[0]
    w_in_b = w_in.astype(BF16)
    seg = lambda k: w_in_b[:, :, o[k]:o[k + 1]]
    w_mid = jnp.concatenate([seg(4), seg(5), seg(6), seg(2)], axis=2)
    wq = w_q_b.reshape(depth, Q_LORA, MLA_HEADS, QK_NOPE + QK_ROPE)
    wq = jnp.pad(wq, ((0, 0), (0, 0), (0, 0), (0, QK_PAD - QK_NOPE - QK_ROPE)))
    wq = wq.reshape(depth, Q_LORA, MLA_HEADS * QK_PAD).astype(BF16)
    wkv = w_kv_b.reshape(depth, KV_LORA, MLA_HEADS, QK_NOPE + V_DIM)
    w_uk = wkv[..., :QK_NOPE].reshape(depth, KV_LORA, MLA_HEADS * QK_NOPE).astype(BF16)
    w_uv = wkv[..., QK_NOPE:].reshape(depth, KV_LORA, MLA_HEADS * V_DIM).astype(BF16)
    c = lambda w: w.astype(BF16)
    return dict(w_in=w_in_b, w_zm=seg(3), w_mid=w_mid, w_g=w_in_b[:, :, o[7]:], wq=wq, w_uk=w_uk, w_uv=w_uv,
                w_ssd_o=c(w_ssd_o), w_mla_o=c(w_mla_o), w_out=c(w_out), w_ple=c(w_ple),
                w_pg_down=c(w_pg_down), w_pg_up=c(w_pg_up))


def _in_proj(x, g_norm, wts, layer, g_q, g_kv, cos_t, sin_t):
    h = _rmsnorm(x, g_norm)
    u_a = _mm(h, wts["w_in"], layer, n_out=ZM_COL0)
    u_zm = _mm(h, wts["w_zm"], layer)
    u_g = _mm(h, wts["w_g"], layer)
    u_mid = _mm(h, wts["w_mid"], layer, tm=min(x.shape[0], 512))
    cqn, ckv, ckv_b, kr, krp, dt_raw = _mid_post(u_mid, g_q, g_kv, cos_t, sin_t)
    q = _mm(cqn, wts["wq"], layer, mode="rope_q", extras=(cos_t, sin_t), out_dtype=BF16,
            table_rows=cos_t.shape[0], tm=min(x.shape[0], cos_t.shape[0], 1024))
    return u_a, u_zm, u_g, ckv, ckv_b, kr, krp, dt_raw, q


def _out_proj(x, y_ssd, o_gated, u_g, p_l, wts, layer, g_ple):
    tn = 512
    ys = _mm(y_ssd, wts["w_ssd_o"], layer, mode="sig_mul", extras=(u_g,), tn=tn)
    merged = _mm(o_gated, wts["w_mla_o"], layer, mode="sig_mul_add", extras=(u_g, ys),
                 extra_col0=(D_MODEL // tn, 0), out_dtype=BF16, tn=tn)
    x = _mm(merged, wts["w_out"], layer, mode="residual", extras=(x,), tn=tn)
    ge = _mm(_rmsnorm(x, g_ple), wts["w_pg_down"], layer, mode="gelu", out_dtype=BF16)
    return _ple(ge, p_l, wts["w_pg_up"], wts["w_ple"], layer, x)


def _layer_prompt(x, p_l, tables, lw, wts, layer, batch, seq):
    (g_norm, conv_w, conv_b, dt_bias, a_log, d_skip, g_ssd, g_q, g_kv, g_ple) = lw
    u_a, u_zm, u_g, ckv, ckv_b, kr, krp, dt_raw, q = _in_proj(x, g_norm, wts, layer, g_q, g_kv, *tables)
    xbc = _conv_prompt(u_a, conv_w, conv_b, seq)
    y_ssd, ssm_new = _ssd_prompt(xbc, u_a, dt_raw, dt_bias, a_log, d_skip, g_ssd, batch, seq)
    kcat = _mm(ckv_b, wts["w_uk"], layer, mode="kcat", extras=(krp,), out_dtype=BF16)
    v = _mm(ckv_b, wts["w_uv"], layer, out_dtype=BF16)
    o_gated = _attn_prompt(q, kcat, v, u_zm, batch, seq)
    x = _out_proj(x, y_ssd, o_gated, u_g, p_l, wts, layer, g_ple)
    conv_new = u_a.reshape(batch, seq, -1)[:, seq - (CONV_W - 1):, SSD_INNER:SSD_INNER + CONV_DIM]
    return x, ckv, kr, ssm_new, conv_new


def _layer_sample(x, p_l, tables, lw, wts, conv_state, ssm_states, ssm_stack, cache_ckv, cache_krope_t,
                  page_table, layer):
    (g_norm, conv_w, conv_b, dt_bias, a_log, d_skip, g_ssd, g_q, g_kv, g_ple) = lw
    u_a, u_zm, u_g, ckv, ckv_b, kr, krp, dt_raw, q = _in_proj(x, g_norm, wts, layer, g_q, g_kv, *tables)
    y_ssd, ssm_stack, conv_new = _ssd_step(
        u_a[:, SSD_INNER:SSD_INNER + CONV_DIM], conv_state, ssm_states, ssm_stack, layer, u_a[:, :SSD_INNER],
        dt_raw, conv_w, conv_b, dt_bias, a_log, d_skip, g_ssd)
    qlat = _qlat(q, wts["w_uk"], layer)
    o_lat = _decode_attn(qlat, ckv_b, krp, cache_ckv, cache_krope_t, page_table, layer)
    o_gated = _olat_proj(o_lat.reshape(x.shape[0], -1), wts["w_uv"], layer, u_zm)
    x = _out_proj(x, y_ssd, o_gated, u_g, p_l, wts, layer, g_ple)
    return x, ckv, kr, ssm_stack, conv_new


def kernel(x_prompt, x_sample, p_prompt, p_sample, cache_ckv, cache_krope, state_ssm, state_conv,
           page_table, g_norm, w_in, conv_w, conv_b, dt_bias, a_log, d_skip, g_ssd, w_ssd_o,
           g_q, w_q_b, g_kv, w_kv_b, w_mla_o, w_out, g_ple, w_ple, w_pg_down, w_pg_up, g_final):
    b_p, s_p, _ = x_prompt.shape
    b_s, s_s, _ = x_sample.shape
    assert s_s == 1
    depth = w_in.shape[0]
    past_len = page_table.shape[1] * cache_ckv.shape[2]
    tab_p = _rope_tables(jnp.arange(s_p))
    tab_s = _rope_tables(jnp.full((b_s,), past_len))
    hp = x_prompt.reshape(b_p * s_p, D_MODEL)
    hs = x_sample.reshape(b_s, D_MODEL)
    ssm_states = state_ssm.reshape(depth, b_s, SSD_INNER, SSD_STATE)
    ssm_stack = jnp.zeros_like(ssm_states)
    cache_krope_t = cache_krope.swapaxes(2, 3)
    outs_p, outs_s = [], []
    wts = _prep_weights(w_in, w_q_b, w_kv_b, w_ssd_o, w_mla_o, w_out, w_ple, w_pg_down, w_pg_up)
    for i in range(depth):
        lw = (g_norm[i], conv_w[i], conv_b[i], dt_bias[i], a_log[i], d_skip[i], g_ssd[i], g_q[i],
              g_kv[i], g_ple[i])
        hp, *o_p = _layer_prompt(hp, p_prompt[i].reshape(b_p * s_p, PLE_DIM), tab_p, lw, wts, i, b_p, s_p)
        hs, *o_s = _layer_sample(hs, p_sample[i].reshape(b_s, PLE_DIM), tab_s, lw, wts, state_conv[i],
                                 ssm_states, ssm_stack, cache_ckv, cache_krope_t, page_table, i)
        ssm_stack = o_s[2]
        outs_p.append(o_p)
        outs_s.append(o_s)
    y_prompt = _rmsnorm(hp, g_final, F32).reshape(b_p, s_p, D_MODEL)
    y_sample = _rmsnorm(hs, g_final, F32).reshape(b_s, s_s, D_MODEL)
    stack = lambda outs, k, shape: jnp.stack([o[k] for o in outs]).reshape((depth,) + shape)
    return (y_prompt, y_sample,
            stack(outs_p, 0, (b_p, s_p, KV_LORA)), stack(outs_p, 1, (b_p, s_p, QK_ROPE)),
            stack(outs_p, 2, (b_p, SSD_HEADS, SSD_HEADDIM, SSD_STATE)),
            stack(outs_p, 3, (b_p, CONV_W - 1, CONV_DIM)),
            stack(outs_s, 0, (b_s, s_s, KV_LORA)), stack(outs_s, 1, (b_s, s_s, QK_ROPE)),
            ssm_stack.reshape(depth, b_s, SSD_HEADS, SSD_HEADDIM, SSD_STATE),
            stack(outs_s, 3, (b_s, CONV_W - 1, CONV_DIM)))
```
